```python
import jax, jax.numpy as jnp
from jax import lax
import numpy as np

D_MODEL = 2048
BATCH = 8
SEQ = 4096
DEPTH = 1

HG_HEADS = 8
HG_DK = 128
HG_DV = 128
HG_CHUNK = 32
HG_WIDTH = HG_HEADS * HG_DK
HG_VWIDTH = HG_HEADS * HG_DV
NSA_HEADS = 8
NSA_KV_GROUPS = 2
NSA_HD = 128
NSA_WIDTH = NSA_HEADS * NSA_HD
NSA_KV_WIDTH = NSA_KV_GROUPS * NSA_HD
CMP_BLOCK = 32
CMP_STRIDE = 16
CMP_HIDDEN = 256
SEL_BLOCK = 64
SEL_TOPK = 16
SEL_Q_BLOCK = 64
WINDOW = 512
WIN_Q_BLOCK = 128
ROPE_THETA = 10000.0
FORCED_SCORE = 1e6
NEG_INF = -1e30
MAX_POS_OFFSET = 1024
PEER_HEADS = 8
PEER_NKEYS = 128
PEER_NEXPERTS = PEER_NKEYS * PEER_NKEYS
PEER_DKEY = 256
PEER_TOPK = 16
PEER_TOKEN_BLOCK = 128
IN_SPLITS = (HG_WIDTH, HG_WIDTH, HG_VWIDTH, HG_VWIDTH, NSA_WIDTH, 6 * NSA_KV_WIDTH, 3 * NSA_HEADS, D_MODEL, D_MODEL)
D_IN = sum(IN_SPLITS)
DEEPNORM_ALPHA = (2 * DEPTH) ** 0.25
DEEPNORM_BETA = (8 * DEPTH) ** -0.25

kernel_name = "hgrn2_nsa_peer_hybrid_block"


def layer_norm(x, g, b, eps=1e-5):
    xf = x.astype(jnp.float32)
    mu = jnp.mean(xf, axis=-1, keepdims=True)
    var = jnp.mean(jnp.square(xf - mu), axis=-1, keepdims=True)
    return ((xf - mu) * lax.rsqrt(var + eps) * g.astype(jnp.float32) + b.astype(jnp.float32)).astype(x.dtype)


def rms_norm(x, g, eps=1e-6):
    xf = x.astype(jnp.float32)
    return (xf * lax.rsqrt(jnp.mean(xf * xf, axis=-1, keepdims=True) + eps) * g.astype(jnp.float32)).astype(x.dtype)


def rope(x, pos):
    d = x.shape[-1]
    inv = ROPE_THETA ** (-jnp.arange(0, d, 2, dtype=jnp.float32) / d)
    ang = pos.astype(jnp.float32)[..., None] * inv
    cos = jnp.cos(ang)[:, :, None, :]
    sin = jnp.sin(ang)[:, :, None, :]
    x1, x2 = jnp.split(x.astype(jnp.float32), 2, axis=-1)
    return jnp.concatenate([x1 * cos - x2 * sin, x2 * cos + x1 * sin], axis=-1).astype(x.dtype)


def hgrn2_mixer(q, fz, i, g, lb, norm_g):
    B, S, _ = q.shape
    n = S // HG_CHUNK
    lbf = lb.astype(jnp.float32)
    zf = fz.astype(jnp.float32)
    logf = jnp.log(lbf + (1.0 - lbf) * jax.nn.sigmoid(zf))
    k = (1.0 - lbf) * jax.nn.sigmoid(-zf)
    qf = jax.nn.silu(q.astype(jnp.float32))
    v = i.astype(jnp.float32)

    def to_chunks(t, d):
        return t.reshape(B, n, HG_CHUNK, HG_HEADS, d).transpose(1, 0, 3, 2, 4)

    xs = (to_chunks(qf, HG_DK), to_chunks(k, HG_DK), to_chunks(v, HG_DV), to_chunks(logf, HG_DK))
    causal = jnp.tril(jnp.ones((HG_CHUNK, HG_CHUNK), dtype=bool))

    def step(state, inp):
        qc, kc, vc, lfc = inp
        b = jnp.cumsum(lfc, axis=-2)
        b_last = b[..., -1:, :]
        q_dec = qc * jnp.exp(b)
        a = jnp.einsum('bhcd,bhsd->bhcs', q_dec, kc * jnp.exp(-b))
        a = jnp.where(causal, a, 0.0)
        o = jnp.einsum('bhcs,bhse->bhce', a, vc) + jnp.einsum('bhcd,bhde->bhce', q_dec, state)
        state = jnp.exp(b_last)[..., 0, :, None] * state + jnp.einsum('bhsd,bhse->bhde', kc * jnp.exp(b_last - b), vc)
        return state, o

    s0 = jnp.zeros((B, HG_HEADS, HG_DK, HG_DV), jnp.float32)
    _, o = lax.scan(step, s0, xs)
    o = o.transpose(1, 0, 3, 2, 4).reshape(B, S, HG_HEADS, HG_DV)
    o = rms_norm(o, norm_g.reshape(HG_HEADS, HG_DV)).reshape(B, S, HG_VWIDTH)
    o = o * jax.nn.silu(g.astype(jnp.float32))
    return o.astype(q.dtype)


def nsa_mixer(q, kv, gates, pos, cmp_pe, cmp_w1, cmp_w2):
    B, S = q.shape[:2]
    G, R, hd = NSA_KV_GROUPS, NSA_HEADS // NSA_KV_GROUPS, NSA_HD
    scale = hd ** -0.5
    t_idx = jnp.arange(S)
    qg = rope(q, pos).astype(jnp.float32).reshape(B, S, G, R, hd)
    k_cmp, v_cmp, k_slc, v_slc, k_win, v_win = [kv[:, :, c] for c in range(6)]

    n_cmp = (S - CMP_BLOCK) // CMP_STRIDE + 1
    blk = jnp.arange(n_cmp)[:, None] * CMP_STRIDE + jnp.arange(CMP_BLOCK)[None, :]

    def compress(t, pe, w1, w2):
        tb = t[:, blk] + pe[None, None, :, None, :]
        tb = tb.transpose(0, 1, 3, 2, 4).reshape(B, n_cmp, G, CMP_BLOCK * hd)
        return (jax.nn.gelu(tb @ w1) @ w2).astype(jnp.float32)

    kc = compress(rope(k_cmp, pos), cmp_pe[0], cmp_w1[0], cmp_w2[0])
    vc = compress(v_cmp, cmp_pe[1], cmp_w1[1], cmp_w2[1])
    s_c = jnp.einsum('bsgrd,bjgd->bgrsj', qg, kc) * scale
    cmp_mask = (jnp.arange(n_cmp) * CMP_STRIDE + CMP_BLOCK - 1)[None, :] <= t_idx[:, None]
    p_cmp = jnp.where(cmp_mask, jax.nn.softmax(jnp.where(cmp_mask, s_c, NEG_INF), axis=-1), 0.0)
    o_cmp = jnp.einsum('bgrsj,bjge->bsgre', p_cmp, vc)

    n_slc = S // SEL_BLOCK
    n_top = min(SEL_TOPK, n_slc)
    cstart = jnp.arange(n_cmp) * CMP_STRIDE
    sstart = jnp.arange(n_slc) * SEL_BLOCK
    overlap = ((cstart[:, None] < sstart[None, :] + SEL_BLOCK) &
               (cstart[:, None] + CMP_BLOCK > sstart[None, :])).astype(jnp.float32)
    imp = jnp.einsum('bgrsj,jn->bgsn', p_cmp, overlap)
    cur = (t_idx // SEL_BLOCK)[:, None]
    blk_id = jnp.arange(n_slc)[None, :]
    forced = (blk_id == 0) | (blk_id == cur) | (blk_id == cur - 1)
    score = jnp.where(blk_id <= cur, jnp.where(forced, FORCED_SCORE, imp), -1.0)
    top_val, top_idx = lax.top_k(score, n_top)
    top_ok = top_val >= 0.0

    k_s = rope(k_slc, pos).astype(jnp.float32).reshape(B, n_slc, SEL_BLOCK, G, hd).transpose(0, 3, 1, 2, 4)
    v_s = v_slc.astype(jnp.float32).reshape(B, n_slc, SEL_BLOCK, G, hd).transpose(0, 3, 1, 2, 4)
    nq = S // SEL_Q_BLOCK
    q_ch = qg.reshape(B, nq, SEL_Q_BLOCK, G, R, hd).transpose(1, 0, 2, 3, 4, 5)
    idx_ch = top_idx.reshape(B, G, nq, SEL_Q_BLOCK, n_top).transpose(2, 0, 1, 3, 4)
    ok_ch = top_ok.reshape(B, G, nq, SEL_Q_BLOCK, n_top).transpose(2, 0, 1, 3, 4)
    t_ch = t_idx.reshape(nq, SEL_Q_BLOCK)
    bi = jnp.arange(B)[:, None, None, None]
    gi = jnp.arange(G)[None, :, None, None]

    def sel_block(args):
        qc, ic, okc, tc = args
        kg = k_s[bi, gi, ic]
        vg = v_s[bi, gi, ic]
        kpos = ic[..., None] * SEL_BLOCK + jnp.arange(SEL_BLOCK)
        m = (okc[..., None] & (kpos <= tc[None, None, :, None, None]))[:, :, None]
        sc = jnp.einsum('bqgrd,bgqnkd->bgrqnk', qc, kg) * scale
        sc = jnp.where(m, sc, NEG_INF).reshape(B, G, R, SEL_Q_BLOCK, n_top * SEL_BLOCK)
        p = jax.nn.softmax(sc, axis=-1).reshape(B, G, R, SEL_Q_BLOCK, n_top, SEL_BLOCK)
        return jnp.einsum('bgrqnk,bgqnke->bqgre', p, vg)

    o_slc = lax.map(sel_block, (q_ch, idx_ch, ok_ch, t_ch))
    o_slc = o_slc.transpose(1, 0, 2, 3, 4, 5).reshape(B, S, G, R, hd)

    nqb = S // WIN_Q_BLOCK
    span = WIN_Q_BLOCK + WINDOW
    band = jnp.arange(nqb)[:, None] * WIN_Q_BLOCK + jnp.arange(span)[None, :]
    pad = ((0, 0), (WINDOW, 0), (0, 0), (0, 0))
    kw = jnp.pad(rope(k_win, pos).astype(jnp.float32), pad)[:, band]
    vw = jnp.pad(v_win.astype(jnp.float32), pad)[:, band]
    qw = qg.reshape(B, nqb, WIN_Q_BLOCK, G, R, hd)
    tq = jnp.arange(nqb)[:, None] * WIN_Q_BLOCK + jnp.arange(WIN_Q_BLOCK)[None, :]
    kp = band - WINDOW
    dist = tq[:, :, None] - kp[:, None, :]
    wmask = (dist >= 0) & (dist < WINDOW) & (kp[:, None, :] >= 0)
    sw = jnp.einsum('bnqgrd,bnkgd->bngrqk', qw, kw) * scale
    pw = jax.nn.softmax(jnp.where(wmask[None, :, None, None], sw, NEG_INF), axis=-1)
    o_win = jnp.einsum('bngrqk,bnkge->bnqgre', pw, vw).reshape(B, S, G, R, hd)

    gt = jax.nn.sigmoid(gates.astype(jnp.float32)).reshape(B, S, G, R, 3)
    o = gt[..., 0:1] * o_cmp + gt[..., 1:2] * o_slc + gt[..., 2:3] * o_win
    return o.reshape(B, S, NSA_WIDTH).astype(q.dtype)


def peer_ffn(x, wq, subkeys, u, v):
    B, S, D = x.shape
    T = B * S
    xt = x.reshape(T, D)
    q = (xt @ wq).astype(jnp.float32).reshape(T, PEER_HEADS, 2, PEER_DKEY // 2)
    s = jnp.einsum('thcd,hcnd->thcn', q, subkeys.astype(jnp.float32))
    hv, hi = lax.top_k(s, PEER_TOPK)
    cand_s = (hv[:, :, 0, :, None] + hv[:, :, 1, None, :]).reshape(T, PEER_HEADS, PEER_TOPK * PEER_TOPK)
    cand_i = (hi[:, :, 0, :, None] * PEER_NKEYS + hi[:, :, 1, None, :]).reshape(T, PEER_HEADS, PEER_TOPK * PEER_TOPK)
    top_s, sel = lax.top_k(cand_s, PEER_TOPK)
    expert = jnp.take_along_axis(cand_i, sel, axis=-1)
    gate = jax.nn.softmax(top_s, axis=-1)
    nb = T // PEER_TOKEN_BLOCK

    def block(args):
        xb, eb, gb = args
        h = jax.nn.gelu(jnp.einsum('td,thkd->thk', xb, u[eb]).astype(jnp.float32), approximate=False)
        w = (gb * h).astype(xb.dtype)
        return jnp.einsum('thk,thkd->td', w, v[eb])

    out = lax.map(block, (xt.reshape(nb, PEER_TOKEN_BLOCK, D),
                          expert.reshape(nb, PEER_TOKEN_BLOCK, PEER_HEADS, PEER_TOPK),
                          gate.reshape(nb, PEER_TOKEN_BLOCK, PEER_HEADS, PEER_TOPK)))
    return out.reshape(B, S, D)


def setup_inputs(seed: int = 0) -> dict:
    key = jax.random.key(seed)
    ks = jax.random.split(key, 24)
    f32 = jnp.float32

    def nrm(k, shape, s):
        return jax.random.normal(k, shape, f32) * s

    x = nrm(ks[0], (BATCH, SEQ, D_MODEL), 1.0)
    offs = jax.random.randint(ks[1], (BATCH, 1), 0, MAX_POS_OFFSET, dtype=jnp.int32)
    positions = offs + jnp.arange(SEQ, dtype=jnp.int32)[None, :]
    return {
        "x": x,
        "positions": positions,
        "ln_in_g": 1.0 + nrm(ks[2], (D_MODEL,), 0.01),
        "ln_in_b": nrm(ks[3], (D_MODEL,), 0.01),
        "w_in": nrm(ks[4], (DEPTH, D_MODEL, D_IN), D_MODEL ** -0.5),
        "hgrn_lb_logits": nrm(ks[5], (DEPTH + 1, HG_WIDTH), 0.5),
        "hgrn_norm_g": 1.0 + nrm(ks[6], (DEPTH, HG_VWIDTH), 0.01),
        "cmp_pe": nrm(ks[7], (DEPTH, 2, CMP_BLOCK, NSA_HD), 0.1),
        "cmp_w1": nrm(ks[8], (DEPTH, 2, CMP_BLOCK * NSA_HD, CMP_HIDDEN), (CMP_BLOCK * NSA_HD) ** -0.5),
        "cmp_w2": nrm(ks[9], (DEPTH, 2, CMP_HIDDEN, NSA_HD), CMP_HIDDEN ** -0.5),
        "w_up_hgrn": nrm(ks[10], (DEPTH, HG_VWIDTH, D_MODEL), HG_VWIDTH ** -0.5),
        "w_up_nsa": nrm(ks[11], (DEPTH, NSA_WIDTH, D_MODEL), NSA_WIDTH ** -0.5),
        "w_out": nrm(ks[12], (DEPTH, D_MODEL, D_MODEL), DEEPNORM_BETA * D_MODEL ** -0.5),
        "ln1_g": 1.0 + nrm(ks[13], (DEPTH, D_MODEL), 0.01),
        "ln1_b": nrm(ks[14], (DEPTH, D_MODEL), 0.01),
        "peer_wq": nrm(ks[15], (DEPTH, D_MODEL, PEER_HEADS * PEER_DKEY), D_MODEL ** -0.5),
        "peer_subkeys": nrm(ks[16], (DEPTH, PEER_HEADS, 2, PEER_NKEYS, PEER_DKEY // 2), (PEER_DKEY // 2) ** -0.5),
        "peer_u": nrm(ks[17], (DEPTH, PEER_NEXPERTS, D_MODEL), D_MODEL ** -0.5),
        "peer_v": nrm(ks[18], (DEPTH, PEER_NEXPERTS, D_MODEL), DEEPNORM_BETA * PEER_HEADS ** -0.5),
        "ln2_g": 1.0 + nrm(ks[19], (DEPTH, D_MODEL), 0.01),
        "ln2_b": nrm(ks[20], (DEPTH, D_MODEL), 0.01),
    }


def reference(x, positions, ln_in_g, ln_in_b, w_in, hgrn_lb_logits, hgrn_norm_g, cmp_pe, cmp_w1, cmp_w2,
              w_up_hgrn, w_up_nsa, w_out, ln1_g, ln1_b, peer_wq, peer_subkeys, peer_u, peer_v, ln2_g, ln2_b):
    B, S, _ = x.shape
    h = layer_norm(x, ln_in_g, ln_in_b)
    lb_all = jnp.cumsum(jax.nn.softmax(hgrn_lb_logits.astype(jnp.float32), axis=0), axis=0)
    split_at = np.cumsum(IN_SPLITS)[:-1].tolist()
    for layer in range(DEPTH):
        proj = h @ w_in[layer]
        hq, hf, hi, hg, nq, nkv, ngate, gate_a, gate_b = jnp.split(proj, split_at, axis=-1)
        o_a = hgrn2_mixer(hq, hf, hi, hg, lb_all[layer], hgrn_norm_g[layer])
        o_b = nsa_mixer(nq.reshape(B, S, NSA_HEADS, NSA_HD),
                        nkv.reshape(B, S, 6, NSA_KV_GROUPS, NSA_HD),
                        ngate.reshape(B, S, NSA_HEADS, 3), positions,
                        cmp_pe[layer], cmp_w1[layer], cmp_w2[layer])
        merged = jax.nn.sigmoid(gate_a) * (o_a @ w_up_hgrn[layer]) + jax.nn.sigmoid(gate_b) * (o_b @ w_up_nsa[layer])
        mix = merged @ w_out[layer]
        h = layer_norm(DEEPNORM_ALPHA * h + mix, ln1_g[layer], ln1_b[layer])
        ffn = peer_ffn(h, peer_wq[layer], peer_subkeys[layer], peer_u[layer], peer_v[layer])
        h = layer_norm(DEEPNORM_ALPHA * h + ffn, ln2_g[layer], ln2_b[layer])
    return h
```

```python
import functools

import jax
import jax.numpy as jnp
import numpy as np
from jax import lax
from jax.experimental import pallas as pl
from jax.experimental.pallas import tpu as pltpu

D_MODEL = 2048
DEPTH = 1
HG_HEADS = 8
HG_DK = 128
HG_DV = 128
HG_CHUNK = 32
HG_WIDTH = HG_HEADS * HG_DK
HG_VWIDTH = HG_HEADS * HG_DV
NSA_HEADS = 8
NSA_KV_GROUPS = 2
NSA_HD = 128
NSA_WIDTH = NSA_HEADS * NSA_HD
NSA_KV_WIDTH = NSA_KV_GROUPS * NSA_HD
CMP_BLOCK = 32
CMP_STRIDE = 16
CMP_HIDDEN = 256
SEL_BLOCK = 64
SEL_TOPK = 16
SEL_Q_BLOCK = 64
WINDOW = 512
WIN_Q_BLOCK = 128
ROPE_THETA = 10000.0
FORCED_SCORE = 1e6
NEG_INF = -1e30
PEER_HEADS = 8
PEER_NKEYS = 128
PEER_DKEY = 256
PEER_TOPK = 16
PEER_TOKEN_BLOCK = 128
IN_SPLITS = (HG_WIDTH, HG_WIDTH, HG_VWIDTH, HG_VWIDTH, NSA_WIDTH, 6 * NSA_KV_WIDTH, 3 * NSA_HEADS, D_MODEL, D_MODEL)
DEEPNORM_ALPHA = (2 * DEPTH) ** 0.25

LANES = 128
VMEM_LIMIT = 48 * 1024 * 1024


def _ln_mm_kernel(x_ref, g_ref, b_ref, w_ref, o_ref, xn_ref, *, eps):
    @pl.when(pl.program_id(1) == 0)
    def _():
        x = x_ref[...]
        mu = jnp.mean(x, axis=-1, keepdims=True)
        xc = x - mu
        var = jnp.mean(xc * xc, axis=-1, keepdims=True)
        xn_ref[...] = (xc * lax.rsqrt(var + eps) * g_ref[...] + b_ref[...]).astype(xn_ref.dtype)

    o_ref[...] = jnp.dot(xn_ref[...], w_ref[...], preferred_element_type=jnp.float32)


def ln_matmul(x, g, b, w, *, tm=512, tn=640, eps=1e-5):
    M, K = x.shape
    N = w.shape[1]
    assert M % tm == 0 and N % tn == 0
    return pl.pallas_call(
        functools.partial(_ln_mm_kernel, eps=eps),
        grid=(M // tm, N // tn),
        in_specs=[
            pl.BlockSpec((tm, K), lambda i, j: (i, 0)),
            pl.BlockSpec((1, K), lambda i, j: (0, 0)),
            pl.BlockSpec((1, K), lambda i, j: (0, 0)),
            pl.BlockSpec((K, tn), lambda i, j: (0, j)),
        ],
        out_specs=pl.BlockSpec((tm, tn), lambda i, j: (i, j)),
        out_shape=jax.ShapeDtypeStruct((M, N), jnp.float32),
        scratch_shapes=[pltpu.VMEM((tm, K), jnp.bfloat16)],
        compiler_params=pltpu.CompilerParams(
            dimension_semantics=("arbitrary", "arbitrary"), vmem_limit_bytes=VMEM_LIMIT),
        name="ln_matmul",
    )(x, g.reshape(1, K), b.reshape(1, K), w)


def _mm_kernel(x_ref, w_ref, o_ref):
    o_ref[...] = jnp.dot(x_ref[...].astype(jnp.bfloat16), w_ref[...], preferred_element_type=jnp.float32)


def matmul(x, w, *, tm=512, tn=512):
    M, K = x.shape
    N = w.shape[1]
    assert M % tm == 0 and N % tn == 0
    return pl.pallas_call(
        _mm_kernel,
        grid=(M // tm, N // tn),
        in_specs=[
            pl.BlockSpec((tm, K), lambda i, j: (i, 0)),
            pl.BlockSpec((K, tn), lambda i, j: (0, j)),
        ],
        out_specs=pl.BlockSpec((tm, tn), lambda i, j: (i, j)),
        out_shape=jax.ShapeDtypeStruct((M, N), jnp.float32),
        compiler_params=pltpu.CompilerParams(
            dimension_semantics=("arbitrary", "arbitrary"), vmem_limit_bytes=VMEM_LIMIT),
        name="matmul",
    )(x, w)


def layer_norm(x, g, b, eps=1e-5):
    mu = jnp.mean(x, axis=-1, keepdims=True)
    var = jnp.mean(jnp.square(x - mu), axis=-1, keepdims=True)
    return (x - mu) * lax.rsqrt(var + eps) * g + b


def rms_norm(x, g, eps=1e-6):
    return x * lax.rsqrt(jnp.mean(x * x, axis=-1, keepdims=True) + eps) * g


def rope(x, pos):
    d = x.shape[-1]
    inv = ROPE_THETA ** (-jnp.arange(0, d, 2, dtype=jnp.float32) / d)
    ang = pos.astype(jnp.float32)[..., None] * inv
    cos = jnp.cos(ang)[:, :, None, :]
    sin = jnp.sin(ang)[:, :, None, :]
    x1, x2 = jnp.split(x, 2, axis=-1)
    return jnp.concatenate([x1 * cos - x2 * sin, x2 * cos + x1 * sin], axis=-1)


def hgrn2_mixer(q, fz, i, g, lb, norm_g):
    B, S, _ = q.shape
    n = S // HG_CHUNK
    logf = jnp.log(lb + (1.0 - lb) * jax.nn.sigmoid(fz))
    k = (1.0 - lb) * jax.nn.sigmoid(-fz)
    qf = jax.nn.silu(q)

    def to_chunks(t, d):
        return t.reshape(B, n, HG_CHUNK, HG_HEADS, d).transpose(1, 0, 3, 2, 4)

    xs = (to_chunks(qf, HG_DK), to_chunks(k, HG_DK), to_chunks(i, HG_DV), to_chunks(logf, HG_DK))
    causal = jnp.tril(jnp.ones((HG_CHUNK, HG_CHUNK), dtype=bool))

    def step(state, inp):
        qc, kc, vc, lfc = inp
        b = jnp.cumsum(lfc, axis=-2)
        b_last = b[..., -1:, :]
        q_dec = qc * jnp.exp(b)
        a = jnp.einsum('bhcd,bhsd->bhcs', q_dec, kc * jnp.exp(-b))
        a = jnp.where(causal, a, 0.0)
        o = jnp.einsum('bhcs,bhse->bhce', a, vc) + jnp.einsum('bhcd,bhde->bhce', q_dec, state)
        state = jnp.exp(b_last)[..., 0, :, None] * state + jnp.einsum('bhsd,bhse->bhde', kc * jnp.exp(b_last - b), vc)
        return state, o

    s0 = jnp.zeros((B, HG_HEADS, HG_DK, HG_DV), jnp.float32)
    _, o = lax.scan(step, s0, xs)
    o = o.transpose(1, 0, 3, 2, 4).reshape(B, S, HG_HEADS, HG_DV)
    o = rms_norm(o, norm_g.reshape(HG_HEADS, HG_DV)).reshape(B, S, HG_VWIDTH)
    return o * jax.nn.silu(g)


def nsa_mixer(q, kv, gates, pos, cmp_pe, cmp_w1, cmp_w2):
    B, S = q.shape[:2]
    G, R, hd = NSA_KV_GROUPS, NSA_HEADS // NSA_KV_GROUPS, NSA_HD
    scale = hd ** -0.5
    t_idx = jnp.arange(S)
    qg = rope(q, pos).reshape(B, S, G, R, hd)
    k_cmp, v_cmp, k_slc, v_slc, k_win, v_win = [kv[:, :, c] for c in range(6)]

    n_cmp = (S - CMP_BLOCK) // CMP_STRIDE + 1
    blk = jnp.arange(n_cmp)[:, None] * CMP_STRIDE + jnp.arange(CMP_BLOCK)[None, :]

    def compress(t, pe, w1, w2):
        tb = t[:, blk] + pe[None, None, :, None, :]
        tb = tb.transpose(0, 1, 3, 2, 4).reshape(B, n_cmp, G, CMP_BLOCK * hd)
        return jax.nn.gelu(tb @ w1) @ w2

    kc = compress(rope(k_cmp, pos), cmp_pe[0], cmp_w1[0], cmp_w2[0])
    vc = compress(v_cmp, cmp_pe[1], cmp_w1[1], cmp_w2[1])
    s_c = jnp.einsum('bsgrd,bjgd->bgrsj', qg, kc) * scale
    cmp_mask = (jnp.arange(n_cmp) * CMP_STRIDE + CMP_BLOCK - 1)[None, :] <= t_idx[:, None]
    p_cmp = jnp.where(cmp_mask, jax.nn.softmax(jnp.where(cmp_mask, s_c, NEG_INF), axis=-1), 0.0)
    o_cmp = jnp.einsum('bgrsj,bjge->bsgre', p_cmp, vc)

    n_slc = S // SEL_BLOCK
    n_top = min(SEL_TOPK, n_slc)
    cstart = jnp.arange(n_cmp) * CMP_STRIDE
    sstart = jnp.arange(n_slc) * SEL_BLOCK
    overlap = ((cstart[:, None] < sstart[None, :] + SEL_BLOCK) &
               (cstart[:, None] + CMP_BLOCK > sstart[None, :])).astype(jnp.float32)
    imp = jnp.einsum('bgrsj,jn->bgsn', p_cmp, overlap)
    cur = (t_idx // SEL_BLOCK)[:, None]
    blk_id = jnp.arange(n_slc)[None, :]
    forced = (blk_id == 0) | (blk_id == cur) | (blk_id == cur - 1)
    score = jnp.where(blk_id <= cur, jnp.where(forced, FORCED_SCORE, imp), -1.0)
    top_val, top_idx = lax.top_k(score, n_top)
    top_ok = top_val >= 0.0

    k_s = rope(k_slc, pos).reshape(B, n_slc, SEL_BLOCK, G, hd).transpose(0, 3, 1, 2, 4)
    v_s = v_slc.reshape(B, n_slc, SEL_BLOCK, G, hd).transpose(0, 3, 1, 2, 4)
    nq = S // SEL_Q_BLOCK
    q_ch = qg.reshape(B, nq, SEL_Q_BLOCK, G, R, hd).transpose(1, 0, 2, 3, 4, 5)
    idx_ch = top_idx.reshape(B, G, nq, SEL_Q_BLOCK, n_top).transpose(2, 0, 1, 3, 4)
    ok_ch = top_ok.reshape(B, G, nq, SEL_Q_BLOCK, n_top).transpose(2, 0, 1, 3, 4)
    t_ch = t_idx.reshape(nq, SEL_Q_BLOCK)
    bi = jnp.arange(B)[:, None, None, None]
    gi = jnp.arange(G)[None, :, None, None]

    def sel_block(args):
        qc, ic, okc, tc = args
        kg = k_s[bi, gi, ic]
        vg = v_s[bi, gi, ic]
        kpos = ic[..., None] * SEL_BLOCK + jnp.arange(SEL_BLOCK)
        m = (okc[..., None] & (kpos <= tc[None, None, :, None, None]))[:, :, None]
        sc = jnp.einsum('bqgrd,bgqnkd->bgrqnk', qc, kg) * scale
        sc = jnp.where(m, sc, NEG_INF).reshape(B, G, R, SEL_Q_BLOCK, n_top * SEL_BLOCK)
        p = jax.nn.softmax(sc, axis=-1).reshape(B, G, R, SEL_Q_BLOCK, n_top, SEL_BLOCK)
        return jnp.einsum('bgrqnk,bgqnke->bqgre', p, vg)

    o_slc = lax.map(sel_block, (q_ch, idx_ch, ok_ch, t_ch))
    o_slc = o_slc.transpose(1, 0, 2, 3, 4, 5).reshape(B, S, G, R, hd)

    nqb = S // WIN_Q_BLOCK
    span = WIN_Q_BLOCK + WINDOW
    band = jnp.arange(nqb)[:, None] * WIN_Q_BLOCK + jnp.arange(span)[None, :]
    pad = ((0, 0), (WINDOW, 0), (0, 0), (0, 0))
    kw = jnp.pad(rope(k_win, pos), pad)[:, band]
    vw = jnp.pad(v_win, pad)[:, band]
    qw = qg.reshape(B, nqb, WIN_Q_BLOCK, G, R, hd)
    tq = jnp.arange(nqb)[:, None] * WIN_Q_BLOCK + jnp.arange(WIN_Q_BLOCK)[None, :]
    kp = band - WINDOW
    dist = tq[:, :, None] - kp[:, None, :]
    wmask = (dist >= 0) & (dist < WINDOW) & (kp[:, None, :] >= 0)
    sw = jnp.einsum('bnqgrd,bnkgd->bngrqk', qw, kw) * scale
    pw = jax.nn.softmax(jnp.where(wmask[None, :, None, None], sw, NEG_INF), axis=-1)
    o_win = jnp.einsum('bngrqk,bnkge->bnqgre', pw, vw).reshape(B, S, G, R, hd)

    gt = jax.nn.sigmoid(gates).reshape(B, S, G, R, 3)
    o = gt[..., 0:1] * o_cmp + gt[..., 1:2] * o_slc + gt[..., 2:3] * o_win
    return o.reshape(B, S, NSA_WIDTH)


def peer_ffn(xt, wq_bf16, subkeys, u, v):
    T, D = xt.shape
    q = matmul(xt, wq_bf16).reshape(T, PEER_HEADS, 2, PEER_DKEY // 2)
    s = jnp.einsum('thcd,hcnd->thcn', q, subkeys)
    hv, hi = lax.top_k(s, PEER_TOPK)
    cand_s = (hv[:, :, 0, :, None] + hv[:, :, 1, None, :]).reshape(T, PEER_HEADS, PEER_TOPK * PEER_TOPK)
    cand_i = (hi[:, :, 0, :, None] * PEER_NKEYS + hi[:, :, 1, None, :]).reshape(T, PEER_HEADS, PEER_TOPK * PEER_TOPK)
    top_s, sel = lax.top_k(cand_s, PEER_TOPK)
    expert = jnp.take_along_axis(cand_i, sel, axis=-1)
    gate = jax.nn.softmax(top_s, axis=-1)
    nb = T // PEER_TOKEN_BLOCK

    def block(args):
        xb, eb, gb = args
        h = jax.nn.gelu(jnp.einsum('td,thkd->thk', xb, u[eb]), approximate=False)
        w = gb * h
        return jnp.einsum('thk,thkd->td', w, v[eb])

    out = lax.map(block, (xt.reshape(nb, PEER_TOKEN_BLOCK, D),
                          expert.reshape(nb, PEER_TOKEN_BLOCK, PEER_HEADS, PEER_TOPK),
                          gate.reshape(nb, PEER_TOKEN_BLOCK, PEER_HEADS, PEER_TOPK)))
    return out


def kernel(x, positions, ln_in_g, ln_in_b, w_in, hgrn_lb_logits, hgrn_norm_g, cmp_pe, cmp_w1, cmp_w2,
           w_up_hgrn, w_up_nsa, w_out, ln1_g, ln1_b, peer_wq, peer_subkeys, peer_u, peer_v, ln2_g, ln2_b):
    B, S, D = x.shape
    T = B * S
    bf16 = jnp.bfloat16
    xt = x.reshape(T, D)
    lb_all = jnp.cumsum(jax.nn.softmax(hgrn_lb_logits, axis=0), axis=0)
    layer = 0

    o = np.cumsum((0,) + IN_SPLITS)
    w = w_in[layer]
    n_gate = IN_SPLITS[6]
    w_r = jnp.concatenate(
        [w[:, o[0]:o[6]], w[:, o[7]:o[9]], w[:, o[6]:o[7]], jnp.zeros((D, LANES - n_gate), w.dtype)], axis=1
    ).astype(bf16)
    proj = ln_matmul(xt, ln_in_g, ln_in_b, w_r)
    h = layer_norm(xt, ln_in_g, ln_in_b)
    c = np.cumsum((0, HG_WIDTH, HG_WIDTH, HG_VWIDTH, HG_VWIDTH, NSA_WIDTH, 6 * NSA_KV_WIDTH, D, D, n_gate))
    hq, hf, hi, hg, nq, nkv, gate_a, gate_b, ngate = [proj[:, c[k]:c[k + 1]].reshape(B, S, -1) for k in range(9)]

    o_a = hgrn2_mixer(hq, hf, hi, hg, lb_all[layer], hgrn_norm_g[layer])
    o_b = nsa_mixer(nq.reshape(B, S, NSA_HEADS, NSA_HD), nkv.reshape(B, S, 6, NSA_KV_GROUPS, NSA_HD),
                    ngate.reshape(B, S, NSA_HEADS, 3), positions, cmp_pe[layer], cmp_w1[layer], cmp_w2[layer])
    up_a = matmul(o_a.reshape(T, HG_VWIDTH), w_up_hgrn[layer].astype(bf16))
    up_b = matmul(o_b.reshape(T, NSA_WIDTH), w_up_nsa[layer].astype(bf16))
    merged = jax.nn.sigmoid(gate_a.reshape(T, D)) * up_a + jax.nn.sigmoid(gate_b.reshape(T, D)) * up_b
    mix = matmul(merged, w_out[layer].astype(bf16))
    h = layer_norm(DEEPNORM_ALPHA * h + mix, ln1_g[layer], ln1_b[layer])
    ffn = peer_ffn(h, peer_wq[layer].astype(bf16), peer_subkeys[layer], peer_u[layer], peer_v[layer])
    h = layer_norm(DEEPNORM_ALPHA * h + ffn.reshape(T, D), ln2_g[layer], ln2_b[layer])
    return h.reshape(B, S, D)
```

```python
import functools

import jax
import jax.numpy as jnp
import numpy as np
from jax import lax
from jax.experimental import pallas as pl
from jax.experimental.pallas import tpu as pltpu

D_MODEL = 2048
DEPTH = 1
HG_HEADS = 8
HG_DK = 128
HG_DV = 128
HG_CHUNK = 32
HG_WIDTH = HG_HEADS * HG_DK
HG_VWIDTH = HG_HEADS * HG_DV
NSA_HEADS = 8
NSA_KV_GROUPS = 2
NSA_HD = 128
NSA_WIDTH = NSA_HEADS * NSA_HD
NSA_KV_WIDTH = NSA_KV_GROUPS * NSA_HD
CMP_BLOCK = 32
CMP_STRIDE = 16
CMP_HIDDEN = 256
SEL_BLOCK = 64
SEL_TOPK = 16
SEL_Q_BLOCK = 64
WINDOW = 512
WIN_Q_BLOCK = 128
ROPE_THETA = 10000.0
FORCED_SCORE = 1e6
NEG_INF = -1e30
PEER_HEADS = 8
PEER_NKEYS = 128
PEER_DKEY = 256
PEER_TOPK = 16
PEER_TOKEN_BLOCK = 128
IN_SPLITS = (HG_WIDTH, HG_WIDTH, HG_VWIDTH, HG_VWIDTH, NSA_WIDTH, 6 * NSA_KV_WIDTH, 3 * NSA_HEADS, D_MODEL, D_MODEL)
DEEPNORM_ALPHA = (2 * DEPTH) ** 0.25

LANES = 128
VMEM_LIMIT = 48 * 1024 * 1024
IN_TN = 1024


def _ln_mm_kernel(x_ref, g_ref, b_ref, w_ref, o_ref, xn_ref, *, eps):
    @pl.when(pl.program_id(1) == 0)
    def _():
        x = x_ref[...]
        mu = jnp.mean(x, axis=-1, keepdims=True)
        xc = x - mu
        var = jnp.mean(xc * xc, axis=-1, keepdims=True)
        xn_ref[...] = (xc * lax.rsqrt(var + eps) * g_ref[...] + b_ref[...]).astype(xn_ref.dtype)

    o_ref[...] = jnp.dot(xn_ref[...], w_ref[...], preferred_element_type=jnp.float32)


def ln_matmul(x, g, b, w, *, tm=512, tn=640, eps=1e-5):
    M, K = x.shape
    N = w.shape[1]
    assert M % tm == 0 and N % tn == 0
    return pl.pallas_call(
        functools.partial(_ln_mm_kernel, eps=eps),
        grid=(M // tm, N // tn),
        in_specs=[
            pl.BlockSpec((tm, K), lambda i, j: (i, 0)),
            pl.BlockSpec((1, K), lambda i, j: (0, 0)),
            pl.BlockSpec((1, K), lambda i, j: (0, 0)),
            pl.BlockSpec((K, tn), lambda i, j: (0, j)),
        ],
        out_specs=pl.BlockSpec((tm, tn), lambda i, j: (i, j)),
        out_shape=jax.ShapeDtypeStruct((M, N), jnp.float32),
        scratch_shapes=[pltpu.VMEM((tm, K), jnp.bfloat16)],
        compiler_params=pltpu.CompilerParams(
            dimension_semantics=("arbitrary", "arbitrary"), vmem_limit_bytes=VMEM_LIMIT),
        name="ln_matmul",
    )(x, g.reshape(1, K), b.reshape(1, K), w)


def _mm_kernel(x_ref, w_ref, o_ref):
    o_ref[...] = jnp.dot(x_ref[...].astype(jnp.bfloat16), w_ref[...], preferred_element_type=jnp.float32)


def matmul(x, w, *, tm=512, tn=512):
    M, K = x.shape
    N = w.shape[1]
    assert M % tm == 0 and N % tn == 0
    return pl.pallas_call(
        _mm_kernel,
        grid=(M // tm, N // tn),
        in_specs=[
            pl.BlockSpec((tm, K), lambda i, j: (i, 0)),
            pl.BlockSpec((K, tn), lambda i, j: (0, j)),
        ],
        out_specs=pl.BlockSpec((tm, tn), lambda i, j: (i, j)),
        out_shape=jax.ShapeDtypeStruct((M, N), jnp.float32),
        compiler_params=pltpu.CompilerParams(
            dimension_semantics=("arbitrary", "arbitrary"), vmem_limit_bytes=VMEM_LIMIT),
        name="matmul",
    )(x, w)


N_KV_PLANES = 6 * NSA_KV_GROUPS
HALF = NSA_HD // 2


def _nsa_prep_kernel(pos_ref, inv_ref, q_ref, kv0_ref, kv1_ref, kv2_ref, qo_ref, kvo_ref, *, scale):
    ang = pos_ref[...] * inv_ref[...]
    lane = lax.broadcasted_iota(jnp.int32, ang.shape, 1)
    cos2 = jnp.cos(ang)
    sin = jnp.sin(ang)
    sin2 = jnp.where(lane < HALF, -sin, sin)

    def rot(x):
        return x * cos2 + pltpu.roll(x, HALF, 1) * sin2

    for h in range(NSA_HEADS):
        sl = slice(h * NSA_HD, (h + 1) * NSA_HD)
        qo_ref[:, sl] = (rot(q_ref[:, sl]) * scale).astype(qo_ref.dtype)
    for p in range(N_KV_PLANES):
        src = (kv0_ref, kv1_ref, kv2_ref)[p // 4]
        x = src[:, (p % 4) * NSA_HD:(p % 4 + 1) * NSA_HD]
        if (p // NSA_KV_GROUPS) % 2 == 0:
            x = rot(x)
        kvo_ref[p] = x.astype(kvo_ref.dtype)


def nsa_prep(proj, pos_f, inv2, q_col, kv_col, *, tm=512):
    T = proj.shape[0]
    assert T % tm == 0 and q_col % NSA_WIDTH == 0 and kv_col % (4 * NSA_HD) == 0
    qb = q_col // NSA_WIDTH
    kb = kv_col // (4 * NSA_HD)
    return pl.pallas_call(
        functools.partial(_nsa_prep_kernel, scale=NSA_HD ** -0.5),
        grid=(T // tm,),
        in_specs=[
            pl.BlockSpec((tm, 1), lambda i: (i, 0)),
            pl.BlockSpec((1, NSA_HD), lambda i: (0, 0)),
            pl.BlockSpec((tm, NSA_WIDTH), lambda i: (i, qb)),
            pl.BlockSpec((tm, 4 * NSA_HD), lambda i: (i, kb)),
            pl.BlockSpec((tm, 4 * NSA_HD), lambda i: (i, kb + 1)),
            pl.BlockSpec((tm, 4 * NSA_HD), lambda i: (i, kb + 2)),
        ],
        out_specs=[
            pl.BlockSpec((tm, NSA_WIDTH), lambda i: (i, 0)),
            pl.BlockSpec((N_KV_PLANES, tm, NSA_HD), lambda i: (0, i, 0)),
        ],
        out_shape=[
            jax.ShapeDtypeStruct((T, NSA_WIDTH), jnp.bfloat16),
            jax.ShapeDtypeStruct((N_KV_PLANES, T, NSA_HD), jnp.bfloat16),
        ],
        compiler_params=pltpu.CompilerParams(dimension_semantics=("arbitrary",), vmem_limit_bytes=VMEM_LIMIT),
        name="nsa_prep",
    )(pos_f, inv2, proj, proj, proj, proj)


def _nsa_cmp_kernel(t2_ref, pe_ref, w1_ref, w2_ref, o_ref):
    t2 = t2_ref[0, 0]
    half = t2.shape[1]
    n = t2.shape[0]
    a = jnp.dot(t2, w1_ref[0, :half, :], preferred_element_type=jnp.float32)
    b = jnp.dot(t2, w1_ref[0, half:, :], preferred_element_type=jnp.float32)
    pe8 = jnp.broadcast_to(pe_ref[0], (8, 2 * half)).astype(jnp.bfloat16)
    bias = jnp.dot(pe8, w1_ref[0], preferred_element_type=jnp.float32)[0:1]
    pre = a + pltpu.roll(b, n - 1, 0) + bias
    hid = jax.nn.gelu(pre)
    o_ref[0, 0, 0] = jnp.dot(hid.astype(jnp.bfloat16), w2_ref[0], preferred_element_type=jnp.float32)


def nsa_compress(kv_planes, pe, w1, w2, B, S):
    ncp = S // CMP_STRIDE
    t2 = kv_planes.reshape(N_KV_PLANES, B, ncp, CMP_STRIDE * NSA_HD)
    kdim = CMP_BLOCK * NSA_HD
    G = NSA_KV_GROUPS
    return pl.pallas_call(
        _nsa_cmp_kernel,
        grid=(2, B, G),
        in_specs=[
            pl.BlockSpec((1, 1, ncp, CMP_STRIDE * NSA_HD), lambda c, b, g: (c * G + g, b, 0, 0)),
            pl.BlockSpec((1, 1, kdim), lambda c, b, g: (c, 0, 0)),
            pl.BlockSpec((1, kdim, CMP_HIDDEN), lambda c, b, g: (c, 0, 0)),
            pl.BlockSpec((1, CMP_HIDDEN, NSA_HD), lambda c, b, g: (c, 0, 0)),
        ],
        out_specs=pl.BlockSpec((1, 1, 1, ncp, NSA_HD), lambda c, b, g: (c, b, g, 0, 0)),
        out_shape=jax.ShapeDtypeStruct((2, B, G, ncp, NSA_HD), jnp.float32),
        compiler_params=pltpu.CompilerParams(
            dimension_semantics=("arbitrary", "arbitrary", "arbitrary"), vmem_limit_bytes=VMEM_LIMIT),
        name="nsa_compress",
    )(t2, pe.reshape(2, 1, kdim), w1.astype(jnp.bfloat16), w2.astype(jnp.bfloat16))


KT = 128


def _flash_step(q4, k, v, ok, carry, reps):
    m_i, l_i, acc = carry
    s = lax.dot_general(q4, k, (((1,), (1,)), ((), ())), preferred_element_type=jnp.float32)
    ok4 = jnp.concatenate([ok] * reps, axis=0)
    sm = jnp.where(ok4, s, NEG_INF)
    m_new = jnp.maximum(m_i, jnp.max(sm, axis=-1, keepdims=True))
    alpha = jnp.exp(m_i - m_new)
    p = jnp.where(ok4, jnp.exp(sm - m_new), 0.0)
    l_new = alpha * l_i + jnp.sum(p, axis=-1, keepdims=True)
    acc = alpha * acc + jnp.dot(p.astype(jnp.bfloat16), v, preferred_element_type=jnp.float32)
    return m_new, l_new, acc


def _nsa_attn_kernel(q_ref, kc_ref, vc_ref, ks_ref, vs_ref, kw_ref, vw_ref, gate_ref, ov_ref, o_ref, *, tq):
    f32, bf16 = jnp.float32, jnp.bfloat16
    R = NSA_HEADS // NSA_KV_GROUPS
    i = pl.program_id(2)
    q = q_ref[...]
    q4 = jnp.concatenate([q[:, r * NSA_HD:(r + 1) * NSA_HD] for r in range(R)], axis=0)
    t = i * tq + lax.broadcasted_iota(jnp.int32, (tq, 1), 0)

    kc = kc_ref[0, 0, 0].astype(bf16)
    vc = vc_ref[0, 0, 0].astype(bf16)
    ncp = kc.shape[0]
    s = lax.dot_general(q4, kc, (((1,), (1,)), ((), ())), preferred_element_type=f32)
    j = lax.broadcasted_iota(jnp.int32, (1, ncp), 1)
    last = jnp.where(j < ncp - 1, j * CMP_STRIDE + (CMP_BLOCK - 1), jnp.int32(2 ** 30))
    valid = jnp.concatenate([last <= t] * R, axis=0)
    sm = jnp.where(valid, s, NEG_INF)
    e = jnp.where(valid, jnp.exp(sm - jnp.max(sm, axis=-1, keepdims=True)), 0.0)
    den = jnp.sum(e, axis=-1, keepdims=True)
    p = e * jnp.where(den > 0.0, 1.0 / den, 0.0)
    o_cmp = jnp.dot(p.astype(bf16), vc, preferred_element_type=f32)
    psum = p[0:tq]
    for r in range(1, R):
        psum = psum + p[r * tq:(r + 1) * tq]
    p_hi = psum.astype(bf16)
    p_lo = (psum - p_hi.astype(f32)).astype(bf16)
    imp = (jnp.dot(p_hi, ov_ref[...], preferred_element_type=f32)
           + jnp.dot(p_lo, ov_ref[...], preferred_element_type=f32))

    n_slc = imp.shape[1]
    n = lax.broadcasted_iota(jnp.int32, (1, n_slc), 1)
    cur = t >> 6
    forced = jnp.where(n == 0, 1.0, 0.0) + jnp.where(n == cur, 1.0, 0.0) + jnp.where(n == cur - 1, 1.0, 0.0)
    sc = jnp.where(n <= cur, jnp.where(forced > 0.0, FORCED_SCORE, imp), -1.0)
    rank = jnp.zeros(sc.shape, f32)
    for m in range(n_slc):
        col = sc[:, m:m + 1]
        rank = rank + jnp.where(n > m, jnp.where(col >= sc, 1.0, 0.0), jnp.where(col > sc, 1.0, 0.0))
    sel = jnp.where(rank < float(SEL_TOPK), jnp.where(sc >= 0.0, 1.0, 0.0), 0.0).astype(bf16)

    lane = lax.broadcasted_iota(jnp.int32, (1, KT), 1)
    e_row = lax.broadcasted_iota(jnp.int32, (n_slc, KT), 0)
    e_lane = lax.broadcasted_iota(jnp.int32, (n_slc, KT), 1)
    init = (jnp.full((R * tq, 1), NEG_INF, f32), jnp.zeros((R * tq, 1), f32), jnp.zeros((R * tq, NSA_HD), f32))

    def slc_body(kb, carry):
        k0 = pl.multiple_of(kb * KT, KT)
        expand = jnp.where(e_row == kb * (KT // SEL_BLOCK) + e_lane // SEL_BLOCK, 1.0, 0.0).astype(bf16)
        sel_k = jnp.dot(sel, expand, preferred_element_type=f32)
        ok = jnp.where(k0 + lane <= t, sel_k, 0.0) > 0.5
        return _flash_step(q4, ks_ref[0, pl.ds(k0, KT), :], vs_ref[0, pl.ds(k0, KT), :], ok, carry, R)

    n_kt = (i * tq + tq) // KT
    _, l_s, acc_s = lax.fori_loop(0, n_kt, slc_body, init)
    o_slc = acc_s / l_s

    def win_body(kb, carry):
        k0 = pl.multiple_of(kb * KT, KT)
        dist = t - (k0 + lane)
        ok = dist.astype(jnp.uint32) < jnp.uint32(WINDOW)
        return _flash_step(q4, kw_ref[0, pl.ds(k0, KT), :], vw_ref[0, pl.ds(k0, KT), :], ok, carry, R)

    first_kt = jnp.maximum(i * tq - WINDOW, 0) // KT
    _, l_w, acc_w = lax.fori_loop(first_kt, n_kt, win_body, init)
    o_win = acc_w / l_w

    gt = jax.nn.sigmoid(gate_ref[...])
    for r in range(R):
        rows = slice(r * tq, (r + 1) * tq)
        o = (gt[:, 3 * r:3 * r + 1] * o_cmp[rows] + gt[:, 3 * r + 1:3 * r + 2] * o_slc[rows]
             + gt[:, 3 * r + 2:3 * r + 3] * o_win[rows])
        o_ref[:, r * NSA_HD:(r + 1) * NSA_HD] = o.astype(o_ref.dtype)


def nsa_attention(q_r, kv_planes, cmp_kv, proj, gate_col, B, S, *, tq=128):
    T = B * S
    G = NSA_KV_GROUPS
    R = NSA_HEADS // G
    ncp = S // CMP_STRIDE
    n_slc = S // SEL_BLOCK
    assert S % tq == 0 and tq % KT == 0 and gate_col % LANES == 0
    nqt = S // tq
    gb = gate_col // LANES
    cstart = np.arange(ncp)[:, None] * CMP_STRIDE
    sstart = np.arange(n_slc)[None, :] * SEL_BLOCK
    overlap = jnp.asarray((cstart < sstart + SEL_BLOCK) & (cstart + CMP_BLOCK > sstart), jnp.bfloat16)

    def plane(c):
        return pl.BlockSpec((1, S, NSA_HD), lambda b, g, i: (c * G + g, b, 0))

    def cmp_spec(c):
        return pl.BlockSpec((1, 1, 1, ncp, NSA_HD), lambda b, g, i: (c, b, g, 0, 0))

    return pl.pallas_call(
        functools.partial(_nsa_attn_kernel, tq=tq),
        grid=(B, G, nqt),
        in_specs=[
            pl.BlockSpec((tq, R * NSA_HD), lambda b, g, i: (b * nqt + i, g)),
            cmp_spec(0), cmp_spec(1),
            plane(2), plane(3), plane(4), plane(5),
            pl.BlockSpec((tq, LANES), lambda b, g, i: (b * nqt + i, gb + g)),
            pl.BlockSpec((ncp, n_slc), lambda b, g, i: (0, 0)),
        ],
        out_specs=pl.BlockSpec((tq, R * NSA_HD), lambda b, g, i: (b * nqt + i, g)),
        out_shape=jax.ShapeDtypeStruct((T, NSA_WIDTH), jnp.bfloat16),
        compiler_params=pltpu.CompilerParams(
            dimension_semantics=("arbitrary", "arbitrary", "arbitrary"), vmem_limit_bytes=VMEM_LIMIT),
        name="nsa_attention",
    )(q_r, cmp_kv, cmp_kv, kv_planes, kv_planes, kv_planes, kv_planes, proj, overlap)


def layer_norm(x, g, b, eps=1e-5):
    mu = jnp.mean(x, axis=-1, keepdims=True)
    var = jnp.mean(jnp.square(x - mu), axis=-1, keepdims=True)
    return (x - mu) * lax.rsqrt(var + eps) * g + b


def rms_norm(x, g, eps=1e-6):
    return x * lax.rsqrt(jnp.mean(x * x, axis=-1, keepdims=True) + eps) * g


def rope(x, pos):
    d = x.shape[-1]
    inv = ROPE_THETA ** (-jnp.arange(0, d, 2, dtype=jnp.float32) / d)
    ang = pos.astype(jnp.float32)[..., None] * inv
    cos = jnp.cos(ang)[:, :, None, :]
    sin = jnp.sin(ang)[:, :, None, :]
    x1, x2 = jnp.split(x, 2, axis=-1)
    return jnp.concatenate([x1 * cos - x2 * sin, x2 * cos + x1 * sin], axis=-1)


def hgrn2_mixer(q, fz, i, g, lb, norm_g):
    B, S, _ = q.shape
    n = S // HG_CHUNK
    logf = jnp.log(lb + (1.0 - lb) * jax.nn.sigmoid(fz))
    k = (1.0 - lb) * jax.nn.sigmoid(-fz)
    qf = jax.nn.silu(q)

    def to_chunks(t, d):
        return t.reshape(B, n, HG_CHUNK, HG_HEADS, d).transpose(1, 0, 3, 2, 4)

    xs = (to_chunks(qf, HG_DK), to_chunks(k, HG_DK), to_chunks(i, HG_DV), to_chunks(logf, HG_DK))
    causal = jnp.tril(jnp.ones((HG_CHUNK, HG_CHUNK), dtype=bool))

    def step(state, inp):
        qc, kc, vc, lfc = inp
        b = jnp.cumsum(lfc, axis=-2)
        b_last = b[..., -1:, :]
        q_dec = qc * jnp.exp(b)
        a = jnp.einsum('bhcd,bhsd->bhcs', q_dec, kc * jnp.exp(-b))
        a = jnp.where(causal, a, 0.0)
        o = jnp.einsum('bhcs,bhse->bhce', a, vc) + jnp.einsum('bhcd,bhde->bhce', q_dec, state)
        state = jnp.exp(b_last)[..., 0, :, None] * state + jnp.einsum('bhsd,bhse->bhde', kc * jnp.exp(b_last - b), vc)
        return state, o

    s0 = jnp.zeros((B, HG_HEADS, HG_DK, HG_DV), jnp.float32)
    _, o = lax.scan(step, s0, xs)
    o = o.transpose(1, 0, 3, 2, 4).reshape(B, S, HG_HEADS, HG_DV)
    o = rms_norm(o, norm_g.reshape(HG_HEADS, HG_DV)).reshape(B, S, HG_VWIDTH)
    return o * jax.nn.silu(g)


def nsa_mixer(q, kv, gates, pos, cmp_pe, cmp_w1, cmp_w2):
    B, S = q.shape[:2]
    G, R, hd = NSA_KV_GROUPS, NSA_HEADS // NSA_KV_GROUPS, NSA_HD
    scale = hd ** -0.5
    t_idx = jnp.arange(S)
    qg = rope(q, pos).reshape(B, S, G, R, hd)
    k_cmp, v_cmp, k_slc, v_slc, k_win, v_win = [kv[:, :, c] for c in range(6)]

    n_cmp = (S - CMP_BLOCK) // CMP_STRIDE + 1
    blk = jnp.arange(n_cmp)[:, None] * CMP_STRIDE + jnp.arange(CMP_BLOCK)[None, :]

    def compress(t, pe, w1, w2):
        tb = t[:, blk] + pe[None, None, :, None, :]
        tb = tb.transpose(0, 1, 3, 2, 4).reshape(B, n_cmp, G, CMP_BLOCK * hd)
        return jax.nn.gelu(tb @ w1) @ w2

    kc = compress(rope(k_cmp, pos), cmp_pe[0], cmp_w1[0], cmp_w2[0])
    vc = compress(v_cmp, cmp_pe[1], cmp_w1[1], cmp_w2[1])
    s_c = jnp.einsum('bsgrd,bjgd->bgrsj', qg, kc) * scale
    cmp_mask = (jnp.arange(n_cmp) * CMP_STRIDE + CMP_BLOCK - 1)[None, :] <= t_idx[:, None]
    p_cmp = jnp.where(cmp_mask, jax.nn.softmax(jnp.where(cmp_mask, s_c, NEG_INF), axis=-1), 0.0)
    o_cmp = jnp.einsum('bgrsj,bjge->bsgre', p_cmp, vc)

    n_slc = S // SEL_BLOCK
    n_top = min(SEL_TOPK, n_slc)
    cstart = jnp.arange(n_cmp) * CMP_STRIDE
    sstart = jnp.arange(n_slc) * SEL_BLOCK
    overlap = ((cstart[:, None] < sstart[None, :] + SEL_BLOCK) &
               (cstart[:, None] + CMP_BLOCK > sstart[None, :])).astype(jnp.float32)
    imp = jnp.einsum('bgrsj,jn->bgsn', p_cmp, overlap)
    cur = (t_idx // SEL_BLOCK)[:, None]
    blk_id = jnp.arange(n_slc)[None, :]
    forced = (blk_id == 0) | (blk_id == cur) | (blk_id == cur - 1)
    score = jnp.where(blk_id <= cur, jnp.where(forced, FORCED_SCORE, imp), -1.0)
    top_val, top_idx = lax.top_k(score, n_top)
    top_ok = top_val >= 0.0

    k_s = rope(k_slc, pos).reshape(B, n_slc, SEL_BLOCK, G, hd).transpose(0, 3, 1, 2, 4)
    v_s = v_slc.reshape(B, n_slc, SEL_BLOCK, G, hd).transpose(0, 3, 1, 2, 4)
    nq = S // SEL_Q_BLOCK
    q_ch = qg.reshape(B, nq, SEL_Q_BLOCK, G, R, hd).transpose(1, 0, 2, 3, 4, 5)
    idx_ch = top_idx.reshape(B, G, nq, SEL_Q_BLOCK, n_top).transpose(2, 0, 1, 3, 4)
    ok_ch = top_ok.reshape(B, G, nq, SEL_Q_BLOCK, n_top).transpose(2, 0, 1, 3, 4)
    t_ch = t_idx.reshape(nq, SEL_Q_BLOCK)
    bi = jnp.arange(B)[:, None, None, None]
    gi = jnp.arange(G)[None, :, None, None]

    def sel_block(args):
        qc, ic, okc, tc = args
        kg = k_s[bi, gi, ic]
        vg = v_s[bi, gi, ic]
        kpos = ic[..., None] * SEL_BLOCK + jnp.arange(SEL_BLOCK)
        m = (okc[..., None] & (kpos <= tc[None, None, :, None, None]))[:, :, None]
        sc = jnp.einsum('bqgrd,bgqnkd->bgrqnk', qc, kg) * scale
        sc = jnp.where(m, sc, NEG_INF).reshape(B, G, R, SEL_Q_BLOCK, n_top * SEL_BLOCK)
        p = jax.nn.softmax(sc, axis=-1).reshape(B, G, R, SEL_Q_BLOCK, n_top, SEL_BLOCK)
        return jnp.einsum('bgrqnk,bgqnke->bqgre', p, vg)

    o_slc = lax.map(sel_block, (q_ch, idx_ch, ok_ch, t_ch))
    o_slc = o_slc.transpose(1, 0, 2, 3, 4, 5).reshape(B, S, G, R, hd)

    nqb = S // WIN_Q_BLOCK
    span = WIN_Q_BLOCK + WINDOW
    band = jnp.arange(nqb)[:, None] * WIN_Q_BLOCK + jnp.arange(span)[None, :]
    pad = ((0, 0), (WINDOW, 0), (0, 0), (0, 0))
    kw = jnp.pad(rope(k_win, pos), pad)[:, band]
    vw = jnp.pad(v_win, pad)[:, band]
    qw = qg.reshape(B, nqb, WIN_Q_BLOCK, G, R, hd)
    tq = jnp.arange(nqb)[:, None] * WIN_Q_BLOCK + jnp.arange(WIN_Q_BLOCK)[None, :]
    kp = band - WINDOW
    dist = tq[:, :, None] - kp[:, None, :]
    wmask = (dist >= 0) & (dist < WINDOW) & (kp[:, None, :] >= 0)
    sw = jnp.einsum('bnqgrd,bnkgd->bngrqk', qw, kw) * scale
    pw = jax.nn.softmax(jnp.where(wmask[None, :, None, None], sw, NEG_INF), axis=-1)
    o_win = jnp.einsum('bngrqk,bnkge->bnqgre', pw, vw).reshape(B, S, G, R, hd)

    gt = jax.nn.sigmoid(gates).reshape(B, S, G, R, 3)
    o = gt[..., 0:1] * o_cmp + gt[..., 1:2] * o_slc + gt[..., 2:3] * o_win
    return o.reshape(B, S, NSA_WIDTH)


def nsa_from_proj(proj, positions, cmp_pe, cmp_w1, cmp_w2, B, S, q_col, kv_col, gate_col):
    T = B * S
    pos_f = positions.reshape(T, 1).astype(jnp.float32)
    inv = ROPE_THETA ** (-jnp.arange(0, NSA_HD, 2, dtype=jnp.float32) / NSA_HD)
    inv2 = jnp.concatenate([inv, inv]).reshape(1, NSA_HD)
    q_r, kv_planes = nsa_prep(proj, pos_f, inv2, q_col, kv_col)
    cmp_kv = nsa_compress(kv_planes, cmp_pe, cmp_w1, cmp_w2, B, S)
    return nsa_attention(q_r, kv_planes, cmp_kv, proj, gate_col, B, S)


def peer_ffn(xt, wq_bf16, subkeys, u, v):
    T, D = xt.shape
    q = matmul(xt, wq_bf16).reshape(T, PEER_HEADS, 2, PEER_DKEY // 2)
    s = jnp.einsum('thcd,hcnd->thcn', q, subkeys)
    hv, hi = lax.top_k(s, PEER_TOPK)
    cand_s = (hv[:, :, 0, :, None] + hv[:, :, 1, None, :]).reshape(T, PEER_HEADS, PEER_TOPK * PEER_TOPK)
    cand_i = (hi[:, :, 0, :, None] * PEER_NKEYS + hi[:, :, 1, None, :]).reshape(T, PEER_HEADS, PEER_TOPK * PEER_TOPK)
    top_s, sel = lax.top_k(cand_s, PEER_TOPK)
    expert = jnp.take_along_axis(cand_i, sel, axis=-1)
    gate = jax.nn.softmax(top_s, axis=-1)
    nb = T // PEER_TOKEN_BLOCK

    def block(args):
        xb, eb, gb = args
        h = jax.nn.gelu(jnp.einsum('td,thkd->thk', xb, u[eb]), approximate=False)
        w = gb * h
        return jnp.einsum('thk,thkd->td', w, v[eb])

    out = lax.map(block, (xt.reshape(nb, PEER_TOKEN_BLOCK, D),
                          expert.reshape(nb, PEER_TOKEN_BLOCK, PEER_HEADS, PEER_TOPK),
                          gate.reshape(nb, PEER_TOKEN_BLOCK, PEER_HEADS, PEER_TOPK)))
    return out


def kernel(x, positions, ln_in_g, ln_in_b, w_in, hgrn_lb_logits, hgrn_norm_g, cmp_pe, cmp_w1, cmp_w2,
           w_up_hgrn, w_up_nsa, w_out, ln1_g, ln1_b, peer_wq, peer_subkeys, peer_u, peer_v, ln2_g, ln2_b):
    B, S, D = x.shape
    T = B * S
    bf16 = jnp.bfloat16
    xt = x.reshape(T, D)
    lb_all = jnp.cumsum(jax.nn.softmax(hgrn_lb_logits, axis=0), axis=0)
    layer = 0

    o = np.cumsum((0,) + IN_SPLITS)
    w = w_in[layer]
    n_gate_g = IN_SPLITS[6] // NSA_KV_GROUPS
    gate_tiles = []
    for g in range(NSA_KV_GROUPS):
        gate_tiles += [w[:, o[6] + g * n_gate_g:o[6] + (g + 1) * n_gate_g], jnp.zeros((D, LANES - n_gate_g), w.dtype)]
    n_used = o[6] + 2 * D + NSA_KV_GROUPS * LANES
    n_pad = -n_used % IN_TN
    w_r = jnp.concatenate([w[:, o[0]:o[6]], w[:, o[7]:o[9]]] + gate_tiles + [jnp.zeros((D, n_pad), w.dtype)],
                          axis=1).astype(bf16)
    proj = ln_matmul(xt, ln_in_g, ln_in_b, w_r, tn=IN_TN)
    h = layer_norm(xt, ln_in_g, ln_in_b)
    c = np.cumsum((0, HG_WIDTH, HG_WIDTH, HG_VWIDTH, HG_VWIDTH, NSA_WIDTH, 6 * NSA_KV_WIDTH, D, D))
    hq, hf, hi, hg = [proj[:, c[k]:c[k + 1]].reshape(B, S, -1) for k in range(4)]
    gate_a, gate_b = proj[:, c[6]:c[7]], proj[:, c[7]:c[8]]

    o_a = hgrn2_mixer(hq, hf, hi, hg, lb_all[layer], hgrn_norm_g[layer])
    o_b = nsa_from_proj(proj, positions, cmp_pe[layer], cmp_w1[layer], cmp_w2[layer], B, S,
                        q_col=int(c[4]), kv_col=int(c[5]), gate_col=int(c[8]))
    up_a = matmul(o_a.reshape(T, HG_VWIDTH), w_up_hgrn[layer].astype(bf16))
    up_b = matmul(o_b, w_up_nsa[layer].astype(bf16))
    merged = jax.nn.sigmoid(gate_a.reshape(T, D)) * up_a + jax.nn.sigmoid(gate_b.reshape(T, D)) * up_b
    mix = matmul(merged, w_out[layer].astype(bf16))
    h = layer_norm(DEEPNORM_ALPHA * h + mix, ln1_g[layer], ln1_b[layer])
    ffn = peer_ffn(h, peer_wq[layer].astype(bf16), peer_subkeys[layer], peer_u[layer], peer_v[layer])
    h = layer_norm(DEEPNORM_ALPHA * h + ffn.reshape(T, D), ln2_g[layer], ln2_b[layer])
    return h.reshape(B, S, D)
```

```python
import functools

import jax
import jax.numpy as jnp
import numpy as np
from jax import lax
from jax.experimental import pallas as pl
from jax.experimental.pallas import tpu as pltpu

D_MODEL = 2048
DEPTH = 1
HG_HEADS = 8
HG_DK = 128
HG_DV = 128
HG_CHUNK = 32
HG_WIDTH = HG_HEADS * HG_DK
HG_VWIDTH = HG_HEADS * HG_DV
NSA_HEADS = 8
NSA_KV_GROUPS = 2
NSA_HD = 128
NSA_WIDTH = NSA_HEADS * NSA_HD
NSA_KV_WIDTH = NSA_KV_GROUPS * NSA_HD
CMP_BLOCK = 32
CMP_STRIDE = 16
CMP_HIDDEN = 256
SEL_BLOCK = 64
SEL_TOPK = 16
WINDOW = 512
ROPE_THETA = 10000.0
FORCED_SCORE = 1e6
NEG_INF = -1e30
PEER_HEADS = 8
PEER_NKEYS = 128
PEER_DKEY = 256
PEER_TOPK = 16
IN_SPLITS = (HG_WIDTH, HG_WIDTH, HG_VWIDTH, HG_VWIDTH, NSA_WIDTH, 6 * NSA_KV_WIDTH, 3 * NSA_HEADS, D_MODEL, D_MODEL)
DEEPNORM_ALPHA = (2 * DEPTH) ** 0.25

LANES = 128
VMEM_LIMIT = 48 * 1024 * 1024
IN_TN = 1024
NT_DIMS = (((1,), (1,)), ((), ()))


def _layer_norm(x, g, b, eps=1e-5):
    mu = jnp.mean(x, axis=-1, keepdims=True)
    xc = x - mu
    var = jnp.mean(xc * xc, axis=-1, keepdims=True)
    return xc * lax.rsqrt(var + eps) * g + b


def _ln_mm_kernel(x_ref, g_ref, b_ref, w_ref, o_ref, xn_ref):
    @pl.when(pl.program_id(1) == 0)
    def _():
        xn_ref[...] = _layer_norm(x_ref[...], g_ref[...], b_ref[...]).astype(xn_ref.dtype)

    o_ref[...] = jnp.dot(xn_ref[...], w_ref[...], preferred_element_type=jnp.float32)


def ln_matmul(x, g, b, w, *, tm=512, tn=640):
    M, K = x.shape
    N = w.shape[1]
    assert M % tm == 0 and N % tn == 0
    return pl.pallas_call(
        _ln_mm_kernel,
        grid=(M // tm, N // tn),
        in_specs=[
            pl.BlockSpec((tm, K), lambda i, j: (i, 0)),
            pl.BlockSpec((1, K), lambda i, j: (0, 0)),
            pl.BlockSpec((1, K), lambda i, j: (0, 0)),
            pl.BlockSpec((K, tn), lambda i, j: (0, j)),
        ],
        out_specs=pl.BlockSpec((tm, tn), lambda i, j: (i, j)),
        out_shape=jax.ShapeDtypeStruct((M, N), jnp.float32),
        scratch_shapes=[pltpu.VMEM((tm, K), jnp.bfloat16)],
        compiler_params=pltpu.CompilerParams(
            dimension_semantics=("arbitrary", "arbitrary"), vmem_limit_bytes=VMEM_LIMIT),
        name="ln_matmul",
    )(x, g.reshape(1, K), b.reshape(1, K), w)


def _mm_kernel(x_ref, w_ref, o_ref):
    o_ref[...] = jnp.dot(x_ref[...].astype(jnp.bfloat16), w_ref[...], preferred_element_type=jnp.float32)


def matmul(x, w, *, tm=512, tn=512):
    M, K = x.shape
    N = w.shape[1]
    assert M % tm == 0 and N % tn == 0
    return pl.pallas_call(
        _mm_kernel,
        grid=(M // tm, N // tn),
        in_specs=[
            pl.BlockSpec((tm, K), lambda i, j: (i, 0)),
            pl.BlockSpec((K, tn), lambda i, j: (0, j)),
        ],
        out_specs=pl.BlockSpec((tm, tn), lambda i, j: (i, j)),
        out_shape=jax.ShapeDtypeStruct((M, N), jnp.float32),
        compiler_params=pltpu.CompilerParams(
            dimension_semantics=("arbitrary", "arbitrary"), vmem_limit_bytes=VMEM_LIMIT),
        name="matmul",
    )(x, w)


N_KV_PLANES = 6 * NSA_KV_GROUPS
HALF = NSA_HD // 2


def _nsa_prep_kernel(pos_ref, inv_ref, q_ref, kv0_ref, kv1_ref, kv2_ref, qo_ref, kvo_ref, *, scale):
    ang = pos_ref[...] * inv_ref[...]
    lane = lax.broadcasted_iota(jnp.int32, ang.shape, 1)
    cos2 = jnp.cos(ang)
    sin = jnp.sin(ang)
    sin2 = jnp.where(lane < HALF, -sin, sin)

    def rot(x):
        return x * cos2 + pltpu.roll(x, HALF, 1) * sin2

    for h in range(NSA_HEADS):
        sl = slice(h * NSA_HD, (h + 1) * NSA_HD)
        qo_ref[:, sl] = (rot(q_ref[:, sl]) * scale).astype(qo_ref.dtype)
    for p in range(N_KV_PLANES):
        src = (kv0_ref, kv1_ref, kv2_ref)[p // 4]
        x = src[:, (p % 4) * NSA_HD:(p % 4 + 1) * NSA_HD]
        if (p // NSA_KV_GROUPS) % 2 == 0:
            x = rot(x)
        kvo_ref[p] = x.astype(kvo_ref.dtype)


def nsa_prep(proj, pos_f, inv2, q_col, kv_col, *, tm=512):
    T = proj.shape[0]
    assert T % tm == 0 and q_col % NSA_WIDTH == 0 and kv_col % (4 * NSA_HD) == 0
    qb = q_col // NSA_WIDTH
    kb = kv_col // (4 * NSA_HD)
    return pl.pallas_call(
        functools.partial(_nsa_prep_kernel, scale=NSA_HD ** -0.5),
        grid=(T // tm,),
        in_specs=[
            pl.BlockSpec((tm, 1), lambda i: (i, 0)),
            pl.BlockSpec((1, NSA_HD), lambda i: (0, 0)),
            pl.BlockSpec((tm, NSA_WIDTH), lambda i: (i, qb)),
            pl.BlockSpec((tm, 4 * NSA_HD), lambda i: (i, kb)),
            pl.BlockSpec((tm, 4 * NSA_HD), lambda i: (i, kb + 1)),
            pl.BlockSpec((tm, 4 * NSA_HD), lambda i: (i, kb + 2)),
        ],
        out_specs=[
            pl.BlockSpec((tm, NSA_WIDTH), lambda i: (i, 0)),
            pl.BlockSpec((N_KV_PLANES, tm, NSA_HD), lambda i: (0, i, 0)),
        ],
        out_shape=[
            jax.ShapeDtypeStruct((T, NSA_WIDTH), jnp.bfloat16),
            jax.ShapeDtypeStruct((N_KV_PLANES, T, NSA_HD), jnp.bfloat16),
        ],
        compiler_params=pltpu.CompilerParams(dimension_semantics=("arbitrary",), vmem_limit_bytes=VMEM_LIMIT),
        name="nsa_prep",
    )(pos_f, inv2, proj, proj, proj, proj)


def _nsa_cmp_kernel(t2_ref, pe_ref, w1_ref, w2_ref, o_ref):
    t2 = t2_ref[0, 0]
    half = t2.shape[1]
    n = t2.shape[0]
    a = jnp.dot(t2, w1_ref[0, :half, :], preferred_element_type=jnp.float32)
    b = jnp.dot(t2, w1_ref[0, half:, :], preferred_element_type=jnp.float32)
    pe8 = jnp.broadcast_to(pe_ref[0], (8, 2 * half)).astype(jnp.bfloat16)
    bias = jnp.dot(pe8, w1_ref[0], preferred_element_type=jnp.float32)[0:1]
    pre = a + pltpu.roll(b, n - 1, 0) + bias
    hid = jax.nn.gelu(pre)
    o_ref[0, 0, 0] = jnp.dot(hid.astype(jnp.bfloat16), w2_ref[0], preferred_element_type=jnp.float32)


def nsa_compress(kv_planes, pe, w1, w2, B, S):
    ncp = S // CMP_STRIDE
    t2 = kv_planes.reshape(N_KV_PLANES, B, ncp, CMP_STRIDE * NSA_HD)
    kdim = CMP_BLOCK * NSA_HD
    G = NSA_KV_GROUPS
    return pl.pallas_call(
        _nsa_cmp_kernel,
        grid=(2, B, G),
        in_specs=[
            pl.BlockSpec((1, 1, ncp, CMP_STRIDE * NSA_HD), lambda c, b, g: (c * G + g, b, 0, 0)),
            pl.BlockSpec((1, 1, kdim), lambda c, b, g: (c, 0, 0)),
            pl.BlockSpec((1, kdim, CMP_HIDDEN), lambda c, b, g: (c, 0, 0)),
            pl.BlockSpec((1, CMP_HIDDEN, NSA_HD), lambda c, b, g: (c, 0, 0)),
        ],
        out_specs=pl.BlockSpec((1, 1, 1, ncp, NSA_HD), lambda c, b, g: (c, b, g, 0, 0)),
        out_shape=jax.ShapeDtypeStruct((2, B, G, ncp, NSA_HD), jnp.float32),
        compiler_params=pltpu.CompilerParams(
            dimension_semantics=("arbitrary", "arbitrary", "arbitrary"), vmem_limit_bytes=VMEM_LIMIT),
        name="nsa_compress",
    )(t2, pe.reshape(2, 1, kdim), w1.astype(jnp.bfloat16), w2.astype(jnp.bfloat16))


KT = 128


def _flash_step(q4, k, v, ok, carry, reps):
    m_i, l_i, acc = carry
    s = lax.dot_general(q4, k, NT_DIMS, preferred_element_type=jnp.float32)
    ok4 = jnp.concatenate([ok] * reps, axis=0)
    sm = jnp.where(ok4, s, NEG_INF)
    m_new = jnp.maximum(m_i, jnp.max(sm, axis=-1, keepdims=True))
    alpha = jnp.exp(m_i - m_new)
    p = jnp.where(ok4, jnp.exp(sm - m_new), 0.0)
    l_new = alpha * l_i + jnp.sum(p, axis=-1, keepdims=True)
    acc = alpha * acc + jnp.dot(p.astype(jnp.bfloat16), v, preferred_element_type=jnp.float32)
    return m_new, l_new, acc


def _nsa_attn_kernel(q_ref, kc_ref, vc_ref, ks_ref, vs_ref, kw_ref, vw_ref, gate_ref, ov_ref, o_ref, *, tq):
    f32, bf16 = jnp.float32, jnp.bfloat16
    R = NSA_HEADS // NSA_KV_GROUPS
    i = pl.program_id(2)
    q = q_ref[...]
    q4 = jnp.concatenate([q[:, r * NSA_HD:(r + 1) * NSA_HD] for r in range(R)], axis=0)
    t = i * tq + lax.broadcasted_iota(jnp.int32, (tq, 1), 0)

    kc = kc_ref[0, 0, 0].astype(bf16)
    vc = vc_ref[0, 0, 0].astype(bf16)
    ncp = kc.shape[0]
    s = lax.dot_general(q4, kc, NT_DIMS, preferred_element_type=f32)
    j = lax.broadcasted_iota(jnp.int32, (1, ncp), 1)
    last = jnp.where(j < ncp - 1, j * CMP_STRIDE + (CMP_BLOCK - 1), jnp.int32(2 ** 30))
    valid = jnp.concatenate([last <= t] * R, axis=0)
    sm = jnp.where(valid, s, NEG_INF)
    e = jnp.where(valid, jnp.exp(sm - jnp.max(sm, axis=-1, keepdims=True)), 0.0)
    den = jnp.sum(e, axis=-1, keepdims=True)
    p = e * jnp.where(den > 0.0, 1.0 / den, 0.0)
    o_cmp = jnp.dot(p.astype(bf16), vc, preferred_element_type=f32)
    psum = p[0:tq]
    for r in range(1, R):
        psum = psum + p[r * tq:(r + 1) * tq]
    p_hi = psum.astype(bf16)
    p_lo = (psum - p_hi.astype(f32)).astype(bf16)
    imp = (jnp.dot(p_hi, ov_ref[...], preferred_element_type=f32)
           + jnp.dot(p_lo, ov_ref[...], preferred_element_type=f32))

    n_slc = imp.shape[1]
    n = lax.broadcasted_iota(jnp.int32, (1, n_slc), 1)
    cur = t >> 6
    forced = jnp.where(n == 0, 1.0, 0.0) + jnp.where(n == cur, 1.0, 0.0) + jnp.where(n == cur - 1, 1.0, 0.0)
    sc = jnp.where(n <= cur, jnp.where(forced > 0.0, FORCED_SCORE, imp), -1.0)
    rank = jnp.zeros(sc.shape, f32)
    for m in range(n_slc):
        col = sc[:, m:m + 1]
        rank = rank + jnp.where(n > m, jnp.where(col >= sc, 1.0, 0.0), jnp.where(col > sc, 1.0, 0.0))
    sel = jnp.where(rank < float(SEL_TOPK), jnp.where(sc >= 0.0, 1.0, 0.0), 0.0).astype(bf16)

    lane = lax.broadcasted_iota(jnp.int32, (1, KT), 1)
    e_row = lax.broadcasted_iota(jnp.int32, (n_slc, KT), 0)
    e_lane = lax.broadcasted_iota(jnp.int32, (n_slc, KT), 1)
    init = (jnp.full((R * tq, 1), NEG_INF, f32), jnp.zeros((R * tq, 1), f32), jnp.zeros((R * tq, NSA_HD), f32))

    def slc_body(kb, carry):
        k0 = pl.multiple_of(kb * KT, KT)
        expand = jnp.where(e_row == kb * (KT // SEL_BLOCK) + e_lane // SEL_BLOCK, 1.0, 0.0).astype(bf16)
        sel_k = jnp.dot(sel, expand, preferred_element_type=f32)
        ok = jnp.where(k0 + lane <= t, sel_k, 0.0) > 0.5
        return _flash_step(q4, ks_ref[0, pl.ds(k0, KT), :], vs_ref[0, pl.ds(k0, KT), :], ok, carry, R)

    n_kt = (i * tq + tq) // KT
    _, l_s, acc_s = lax.fori_loop(0, n_kt, slc_body, init)
    o_slc = acc_s / l_s

    def win_body(kb, carry):
        k0 = pl.multiple_of(kb * KT, KT)
        dist = t - (k0 + lane)
        ok = dist.astype(jnp.uint32) < jnp.uint32(WINDOW)
        return _flash_step(q4, kw_ref[0, pl.ds(k0, KT), :], vw_ref[0, pl.ds(k0, KT), :], ok, carry, R)

    first_kt = jnp.maximum(i * tq - WINDOW, 0) // KT
    _, l_w, acc_w = lax.fori_loop(first_kt, n_kt, win_body, init)
    o_win = acc_w / l_w

    gt = jax.nn.sigmoid(gate_ref[...])
    for r in range(R):
        rows = slice(r * tq, (r + 1) * tq)
        o = (gt[:, 3 * r:3 * r + 1] * o_cmp[rows] + gt[:, 3 * r + 1:3 * r + 2] * o_slc[rows]
             + gt[:, 3 * r + 2:3 * r + 3] * o_win[rows])
        o_ref[:, r * NSA_HD:(r + 1) * NSA_HD] = o.astype(o_ref.dtype)


def nsa_attention(q_r, kv_planes, cmp_kv, proj, gate_col, B, S, *, tq=128):
    T = B * S
    G = NSA_KV_GROUPS
    R = NSA_HEADS // G
    ncp = S // CMP_STRIDE
    n_slc = S // SEL_BLOCK
    assert S % tq == 0 and tq % KT == 0 and gate_col % LANES == 0
    nqt = S // tq
    gb = gate_col // LANES
    cstart = np.arange(ncp)[:, None] * CMP_STRIDE
    sstart = np.arange(n_slc)[None, :] * SEL_BLOCK
    overlap = jnp.asarray((cstart < sstart + SEL_BLOCK) & (cstart + CMP_BLOCK > sstart), jnp.bfloat16)

    def plane(c):
        return pl.BlockSpec((1, S, NSA_HD), lambda b, g, i: (c * G + g, b, 0))

    def cmp_spec(c):
        return pl.BlockSpec((1, 1, 1, ncp, NSA_HD), lambda b, g, i: (c, b, g, 0, 0))

    return pl.pallas_call(
        functools.partial(_nsa_attn_kernel, tq=tq),
        grid=(B, G, nqt),
        in_specs=[
            pl.BlockSpec((tq, R * NSA_HD), lambda b, g, i: (b * nqt + i, g)),
            cmp_spec(0), cmp_spec(1),
            plane(2), plane(3), plane(4), plane(5),
            pl.BlockSpec((tq, LANES), lambda b, g, i: (b * nqt + i, gb + g)),
            pl.BlockSpec((ncp, n_slc), lambda b, g, i: (0, 0)),
        ],
        out_specs=pl.BlockSpec((tq, R * NSA_HD), lambda b, g, i: (b * nqt + i, g)),
        out_shape=jax.ShapeDtypeStruct((T, NSA_WIDTH), jnp.bfloat16),
        compiler_params=pltpu.CompilerParams(
            dimension_semantics=("arbitrary", "arbitrary", "arbitrary"), vmem_limit_bytes=VMEM_LIMIT),
        name="nsa_attention",
    )(q_r, cmp_kv, cmp_kv, kv_planes, kv_planes, kv_planes, kv_planes, proj, overlap)


def nsa_from_proj(proj, positions, cmp_pe, cmp_w1, cmp_w2, B, S, q_col, kv_col, gate_col):
    T = B * S
    pos_f = positions.reshape(T, 1).astype(jnp.float32)
    inv = ROPE_THETA ** (-jnp.arange(0, NSA_HD, 2, dtype=jnp.float32) / NSA_HD)
    inv2 = jnp.concatenate([inv, inv]).reshape(1, NSA_HD)
    q_r, kv_planes = nsa_prep(proj, pos_f, inv2, q_col, kv_col)
    cmp_kv = nsa_compress(kv_planes, cmp_pe, cmp_w1, cmp_w2, B, S)
    return nsa_attention(q_r, kv_planes, cmp_kv, proj, gate_col, B, S)


N_TOK_EXPERTS = PEER_HEADS * PEER_TOPK
SUBKEY_DIM = PEER_DKEY // 2
N_HALVES = 2 * PEER_HEADS
CAND_SMALL = 8


def _extract_top(vals, tags, big, count):
    out_v, out_t = [], []
    for _ in range(count):
        mx = jnp.max(vals, axis=0, keepdims=True)
        tag = jnp.min(jnp.where(vals == mx, tags, big), axis=0, keepdims=True)
        out_v.append(mx)
        out_t.append(tag)
        vals = jnp.where(tags == tag, -jnp.inf, vals)
    return out_v, out_t


def _peer_topk_kernel(h_ref, wq_ref, sk_ref, spread_ref, ids_ref, g_ref, q_scr, hv_scr, hi_scr, e_scr, s_scr):
    f32, bf16, i32 = jnp.float32, jnp.bfloat16, jnp.int32
    tm = h_ref.shape[0]
    q = jnp.dot(h_ref[...].astype(bf16), wq_ref[...], preferred_element_type=f32)
    for hc in range(N_HALVES):
        q_scr[hc] = q[:, hc * SUBKEY_DIM:(hc + 1) * SUBKEY_DIM].astype(bf16)
    key_id = lax.broadcasted_iota(i32, (PEER_NKEYS, tm), 0)

    def stage1(hc, carry):
        scores = lax.dot_general(sk_ref[hc], q_scr[hc], NT_DIMS, preferred_element_type=f32)
        vs, ks = _extract_top(scores, key_id, PEER_NKEYS, PEER_TOPK)
        hv_scr[hc] = jnp.concatenate(vs, axis=0)
        hi_scr[hc] = jnp.concatenate(ks, axis=0)
        return carry

    lax.fori_loop(0, N_HALVES, stage1, 0)

    b_all = lax.broadcasted_iota(i32, (PEER_TOPK, tm), 0)
    b_small = lax.broadcasted_iota(i32, (CAND_SMALL, tm), 0)

    def stage2(h, carry):
        v0, v1 = hv_scr[2 * h], hv_scr[2 * h + 1]
        i0, i1 = hi_scr[2 * h], hi_scr[2 * h + 1]
        cs = [v0[0:1] + v1]
        ce = [i0[0:1] * PEER_NKEYS + i1]
        cf = [b_all]
        for a in range(1, CAND_SMALL):
            cs.append(v0[a:a + 1] + v1[0:CAND_SMALL])
            ce.append(i0[a:a + 1] * PEER_NKEYS + i1[0:CAND_SMALL])
            cf.append(a * PEER_TOPK + b_small)
        cs.append(v0[CAND_SMALL:] + v1[0:1])
        ce.append(i0[CAND_SMALL:] * PEER_NKEYS + i1[0:1])
        cf.append((CAND_SMALL + b_small) * PEER_TOPK)
        cand = jnp.concatenate(cs, axis=0)
        eid = jnp.concatenate(ce, axis=0)
        flat = jnp.concatenate(cf, axis=0)
        ts, te = [], []
        for _ in range(PEER_TOPK):
            mx = jnp.max(cand, axis=0, keepdims=True)
            first = jnp.min(jnp.where(cand == mx, flat, PEER_TOPK * PEER_TOPK), axis=0, keepdims=True)
            hit = flat == first
            te.append(jnp.max(jnp.where(hit, eid, -1), axis=0, keepdims=True))
            ts.append(mx)
            cand = jnp.where(hit, -jnp.inf, cand)
        top_s = jnp.concatenate(ts, axis=0)
        ex = jnp.exp(top_s - top_s[0:1])
        rows = pl.ds(pl.multiple_of(h * PEER_TOPK, PEER_TOPK), PEER_TOPK)
        s_scr[rows, :] = ex / jnp.sum(ex, axis=0, keepdims=True)
        e_scr[rows, :] = jnp.concatenate(te, axis=0)
        return carry

    lax.fori_loop(0, PEER_HEADS, stage2, 0)

    ids_ref[...] = e_scr[...].T
    gt = s_scr[...].T
    g_hi = gt.astype(bf16)
    g_lo = (gt - g_hi.astype(f32)).astype(bf16)
    g_ref[...] = (jnp.dot(g_hi, spread_ref[...], preferred_element_type=f32)
                  + jnp.dot(g_lo, spread_ref[...], preferred_element_type=f32))


def peer_topk(h, wq, subkeys, *, tm=256):
    T, D = h.shape
    assert T % tm == 0
    nq = wq.shape[1]
    spread = jnp.asarray(np.arange(2 * N_TOK_EXPERTS)[None, :] == 2 * np.arange(N_TOK_EXPERTS)[:, None], jnp.bfloat16)
    sk = subkeys.reshape(N_HALVES, PEER_NKEYS, SUBKEY_DIM).astype(jnp.bfloat16)
    return pl.pallas_call(
        _peer_topk_kernel,
        grid=(T // tm,),
        in_specs=[
            pl.BlockSpec((tm, D), lambda i: (i, 0)),
            pl.BlockSpec((D, nq), lambda i: (0, 0)),
            pl.BlockSpec((N_HALVES, PEER_NKEYS, SUBKEY_DIM), lambda i: (0, 0, 0)),
            pl.BlockSpec((N_TOK_EXPERTS, 2 * N_TOK_EXPERTS), lambda i: (0, 0)),
        ],
        out_specs=[
            pl.BlockSpec((tm, N_TOK_EXPERTS), lambda i: (i, 0)),
            pl.BlockSpec((tm, 2 * N_TOK_EXPERTS), lambda i: (i, 0)),
        ],
        out_shape=[
            jax.ShapeDtypeStruct((T, N_TOK_EXPERTS), jnp.int32),
            jax.ShapeDtypeStruct((T, 2 * N_TOK_EXPERTS), jnp.float32),
        ],
        scratch_shapes=[
            pltpu.VMEM((N_HALVES, tm, SUBKEY_DIM), jnp.bfloat16),
            pltpu.VMEM((N_HALVES, PEER_TOPK, tm), jnp.float32),
            pltpu.VMEM((N_HALVES, PEER_TOPK, tm), jnp.int32),
            pltpu.VMEM((N_TOK_EXPERTS, tm), jnp.int32),
            pltpu.VMEM((N_TOK_EXPERTS, tm), jnp.float32),
        ],
        compiler_params=pltpu.CompilerParams(dimension_semantics=("arbitrary",), vmem_limit_bytes=VMEM_LIMIT),
        name="peer_topk",
    )(h, wq, sk, spread)


N_SLOTS = 16
PHASE = 4
CHUNK = 128


def _peer_expert_kernel(ids_ref, x_ref, g_ref, lg_ref, lb_ref, tab_ref, o_ref, *scratch):
    bufs, ffn_scr, sems = scratch[:N_SLOTS], scratch[N_SLOTS], scratch[N_SLOTS + 1]
    f32, bf16 = jnp.float32, jnp.bfloat16
    tb, D = x_ref.shape
    half = D // 2
    n_rows = 2 * N_TOK_EXPERTS

    def issue(t, slot, k_lo=0, k_hi=N_TOK_EXPERTS):
        for k in range(k_lo, k_hi):
            pltpu.make_async_copy(tab_ref.at[pl.ds(ids_ref[t, k], 1), :], bufs[slot].at[pl.ds(k, 1), :],
                                  sems.at[slot]).start(priority=k % 2)

    def wait(slot):
        pltpu.make_async_copy(bufs[slot], bufs[slot], sems.at[slot]).wait()

    n_phases = N_SLOTS // PHASE
    for j in range(N_SLOTS - PHASE):
        issue(j, j)

    pad = jnp.zeros((14, CHUNK), f32)
    padw = jnp.zeros((14, n_rows), f32)
    n_chunks = half // CHUNK
    per_chunk = N_TOK_EXPERTS // n_chunks

    def group(it, last):
        t0 = pl.multiple_of(it * N_SLOTS, N_SLOTS)
        for p in range(n_phases):
            slots = list(range(p * PHASE, (p + 1) * PHASE))
            refill = list(range((p - 1) % n_phases * PHASE, ((p - 1) % n_phases + 1) * PHASE))
            prefetch = not (last and p > 0)
            for j in slots:
                wait(j)
            rows = [pl.ds(t0 + j, 1) for j in slots]
            pres = []
            for n, j in enumerate(slots):
                ahead = t0 + refill[n] if p == 0 else t0 + N_SLOTS + refill[n]
                x = x_ref[rows[n], :]
                d = jnp.zeros((16, n_rows), f32)
                for c in range(n_chunks):
                    lo = slice(c * CHUNK, (c + 1) * CHUNK)
                    hi = slice(half + c * CHUNK, half + (c + 1) * CHUNK)
                    x2 = jnp.concatenate([x[:, lo], x[:, hi], pad], axis=0).astype(bf16)
                    wc = pltpu.bitcast(bufs[j][:, lo], bf16)
                    d = d + lax.dot_general(x2, wc, NT_DIMS, preferred_element_type=f32)
                    if prefetch:
                        issue(ahead, refill[n], c * per_chunk, (c + 1) * per_chunk)
                pres.append(d)
            for n, j in enumerate(slots):
                d = pres[n]
                pre = d[0:1] + pltpu.roll(d[1:2], n_rows - 1, 1)
                act = 0.5 * pre * (1.0 + lax.erf(pre * (2.0 ** -0.5)))
                wa = act * g_ref[rows[n], :]
                w2 = jnp.concatenate([wa, pltpu.roll(wa, 1, 1), padw], axis=0).astype(bf16)
                wv = pltpu.bitcast(bufs[j][:, half:], bf16)
                o = jnp.dot(w2, wv, preferred_element_type=f32)
                ffn_scr[rows[n], :half] = o[0:1]
                ffn_scr[rows[n], half:] = o[1:2]

    def body(it, carry):
        group(it, False)
        return carry

    n_groups = tb // N_SLOTS
    lax.fori_loop(0, n_groups - 1, body, 0)
    group(n_groups - 1, True)
    o_ref[...] = _layer_norm(DEEPNORM_ALPHA * x_ref[...] + ffn_scr[...], lg_ref[...], lb_ref[...])


def peer_pack_table(u, v):
    def pack(w):
        bits = lax.bitcast_convert_type(w.astype(jnp.bfloat16), jnp.uint16).astype(jnp.uint32)
        hw = w.shape[1] // 2
        return bits[:, :hw] | (bits[:, hw:] << 16)
    return jnp.concatenate([pack(u), pack(v)], axis=1)


def peer_experts_ln(h, ids, gates, table, ln_g, ln_b, *, tb=256):
    T, D = h.shape
    assert T % tb == 0 and tb % N_SLOTS == 0 and tb >= 2 * N_SLOTS
    return pl.pallas_call(
        _peer_expert_kernel,
        grid=(T // tb,),
        in_specs=[
            pl.BlockSpec((tb, N_TOK_EXPERTS), lambda i: (i, 0), memory_space=pltpu.SMEM),
            pl.BlockSpec((tb, D), lambda i: (i, 0)),
            pl.BlockSpec((tb, 2 * N_TOK_EXPERTS), lambda i: (i, 0)),
            pl.BlockSpec((1, D), lambda i: (0, 0)),
            pl.BlockSpec((1, D), lambda i: (0, 0)),
            pl.BlockSpec(memory_space=pl.ANY),
        ],
        out_specs=pl.BlockSpec((tb, D), lambda i: (i, 0)),
        out_shape=jax.ShapeDtypeStruct((T, D), jnp.float32),
        scratch_shapes=[pltpu.VMEM((N_TOK_EXPERTS, D), jnp.uint32) for _ in range(N_SLOTS)] + [
            pltpu.VMEM((tb, D), jnp.float32),
            pltpu.SemaphoreType.DMA((N_SLOTS,)),
        ],
        compiler_params=pltpu.CompilerParams(dimension_semantics=("arbitrary",), vmem_limit_bytes=VMEM_LIMIT),
        name="peer_experts",
    )(ids, h, gates, ln_g.reshape(1, D), ln_b.reshape(1, D), table)


def rms_norm(x, g, eps=1e-6):
    return x * lax.rsqrt(jnp.mean(x * x, axis=-1, keepdims=True) + eps) * g


def hgrn2_mixer(q, fz, i, g, lb, norm_g):
    B, S, _ = q.shape
    n = S // HG_CHUNK
    logf = jnp.log(lb + (1.0 - lb) * jax.nn.sigmoid(fz))
    k = (1.0 - lb) * jax.nn.sigmoid(-fz)
    qf = jax.nn.silu(q)

    def to_chunks(t, d):
        return t.reshape(B, n, HG_CHUNK, HG_HEADS, d).transpose(1, 0, 3, 2, 4)

    xs = (to_chunks(qf, HG_DK), to_chunks(k, HG_DK), to_chunks(i, HG_DV), to_chunks(logf, HG_DK))
    causal = jnp.tril(jnp.ones((HG_CHUNK, HG_CHUNK), dtype=bool))

    def step(state, inp):
        qc, kc, vc, lfc = inp
        b = jnp.cumsum(lfc, axis=-2)
        b_last = b[..., -1:, :]
        q_dec = qc * jnp.exp(b)
        a = jnp.einsum('bhcd,bhsd->bhcs', q_dec, kc * jnp.exp(-b))
        a = jnp.where(causal, a, 0.0)
        o = jnp.einsum('bhcs,bhse->bhce', a, vc) + jnp.einsum('bhcd,bhde->bhce', q_dec, state)
        state = jnp.exp(b_last)[..., 0, :, None] * state + jnp.einsum('bhsd,bhse->bhde', kc * jnp.exp(b_last - b), vc)
        return state, o

    s0 = jnp.zeros((B, HG_HEADS, HG_DK, HG_DV), jnp.float32)
    _, o = lax.scan(step, s0, xs)
    o = o.transpose(1, 0, 3, 2, 4).reshape(B, S, HG_HEADS, HG_DV)
    o = rms_norm(o, norm_g.reshape(HG_HEADS, HG_DV)).reshape(B, S, HG_VWIDTH)
    return o * jax.nn.silu(g)


def kernel(x, positions, ln_in_g, ln_in_b, w_in, hgrn_lb_logits, hgrn_norm_g, cmp_pe, cmp_w1, cmp_w2,
           w_up_hgrn, w_up_nsa, w_out, ln1_g, ln1_b, peer_wq, peer_subkeys, peer_u, peer_v, ln2_g, ln2_b):
    B, S, D = x.shape
    T = B * S
    bf16 = jnp.bfloat16
    xt = x.reshape(T, D)
    lb_all = jnp.cumsum(jax.nn.softmax(hgrn_lb_logits, axis=0), axis=0)
    layer = 0

    o = np.cumsum((0,) + IN_SPLITS)
    w = w_in[layer]
    n_gate_g = IN_SPLITS[6] // NSA_KV_GROUPS
    gate_tiles = []
    for g in range(NSA_KV_GROUPS):
        gate_tiles += [w[:, o[6] + g * n_gate_g:o[6] + (g + 1) * n_gate_g], jnp.zeros((D, LANES - n_gate_g), w.dtype)]
    n_used = o[6] + 2 * D + NSA_KV_GROUPS * LANES
    n_pad = -n_used % IN_TN
    w_r = jnp.concatenate([w[:, o[0]:o[6]], w[:, o[7]:o[9]]] + gate_tiles + [jnp.zeros((D, n_pad), w.dtype)],
                          axis=1).astype(bf16)
    proj = ln_matmul(xt, ln_in_g, ln_in_b, w_r, tn=IN_TN)
    h = _layer_norm(xt, ln_in_g, ln_in_b)
    c = np.cumsum((0, HG_WIDTH, HG_WIDTH, HG_VWIDTH, HG_VWIDTH, NSA_WIDTH, 6 * NSA_KV_WIDTH, D, D))
    hq, hf, hi, hg = [proj[:, c[k]:c[k + 1]].reshape(B, S, -1) for k in range(4)]
    gate_a, gate_b = proj[:, c[6]:c[7]], proj[:, c[7]:c[8]]

    o_a = hgrn2_mixer(hq, hf, hi, hg, lb_all[layer], hgrn_norm_g[layer])
    o_b = nsa_from_proj(proj, positions, cmp_pe[layer], cmp_w1[layer], cmp_w2[layer], B, S,
                        q_col=int(c[4]), kv_col=int(c[5]), gate_col=int(c[8]))
    up_a = matmul(o_a.reshape(T, HG_VWIDTH), w_up_hgrn[layer].astype(bf16))
    up_b = matmul(o_b, w_up_nsa[layer].astype(bf16))
    merged = jax.nn.sigmoid(gate_a) * up_a + jax.nn.sigmoid(gate_b) * up_b
    mix = matmul(merged, w_out[layer].astype(bf16))
    h = _layer_norm(DEEPNORM_ALPHA * h + mix, ln1_g[layer], ln1_b[layer])

    ids, gates = peer_topk(h, peer_wq[layer].astype(bf16), peer_subkeys[layer])
    table = peer_pack_table(peer_u[layer], peer_v[layer])
    h = peer_experts_ln(h, ids, gates, table, ln2_g[layer], ln2_b[layer])
    return h.reshape(B, S, D)
```

```python
import functools

import jax
import jax.numpy as jnp
import numpy as np
from jax import lax
from jax.experimental import pallas as pl
from jax.experimental.pallas import tpu as pltpu

D_MODEL = 2048
DEPTH = 1
HG_HEADS = 8
HG_DK = 128
HG_DV = 128
HG_CHUNK = 32
HG_WIDTH = HG_HEADS * HG_DK
HG_VWIDTH = HG_HEADS * HG_DV
NSA_HEADS = 8
NSA_KV_GROUPS = 2
NSA_HD = 128
NSA_WIDTH = NSA_HEADS * NSA_HD
NSA_KV_WIDTH = NSA_KV_GROUPS * NSA_HD
CMP_BLOCK = 32
CMP_STRIDE = 16
CMP_HIDDEN = 256
SEL_BLOCK = 64
SEL_TOPK = 16
WINDOW = 512
ROPE_THETA = 10000.0
FORCED_SCORE = 1e6
NEG_INF = -1e30
PEER_HEADS = 8
PEER_NKEYS = 128
PEER_DKEY = 256
PEER_TOPK = 16
IN_SPLITS = (HG_WIDTH, HG_WIDTH, HG_VWIDTH, HG_VWIDTH, NSA_WIDTH, 6 * NSA_KV_WIDTH, 3 * NSA_HEADS, D_MODEL, D_MODEL)
DEEPNORM_ALPHA = (2 * DEPTH) ** 0.25

LANES = 128
VMEM_LIMIT = 48 * 1024 * 1024
IN_TN = 1024
NT_DIMS = (((1,), (1,)), ((), ()))


def _layer_norm(x, g, b, eps=1e-5):
    mu = jnp.mean(x, axis=-1, keepdims=True)
    xc = x - mu
    var = jnp.mean(xc * xc, axis=-1, keepdims=True)
    return xc * lax.rsqrt(var + eps) * g + b


def _ln_mm_kernel(x_ref, g_ref, b_ref, w_ref, o_ref, xn_ref):
    @pl.when(pl.program_id(1) == 0)
    def _():
        xn_ref[...] = _layer_norm(x_ref[...], g_ref[...], b_ref[...]).astype(xn_ref.dtype)

    o_ref[...] = jnp.dot(xn_ref[...], w_ref[...], preferred_element_type=jnp.float32)


def ln_matmul(x, g, b, w, *, tm=512, tn=640):
    M, K = x.shape
    N = w.shape[1]
    assert M % tm == 0 and N % tn == 0
    return pl.pallas_call(
        _ln_mm_kernel,
        grid=(M // tm, N // tn),
        in_specs=[
            pl.BlockSpec((tm, K), lambda i, j: (i, 0)),
            pl.BlockSpec((1, K), lambda i, j: (0, 0)),
            pl.BlockSpec((1, K), lambda i, j: (0, 0)),
            pl.BlockSpec((K, tn), lambda i, j: (0, j)),
        ],
        out_specs=pl.BlockSpec((tm, tn), lambda i, j: (i, j)),
        out_shape=jax.ShapeDtypeStruct((M, N), jnp.float32),
        scratch_shapes=[pltpu.VMEM((tm, K), jnp.bfloat16)],
        compiler_params=pltpu.CompilerParams(
            dimension_semantics=("arbitrary", "arbitrary"), vmem_limit_bytes=VMEM_LIMIT),
        name="ln_matmul",
    )(x, g.reshape(1, K), b.reshape(1, K), w)


N_KV_PLANES = 6 * NSA_KV_GROUPS
HALF = NSA_HD // 2


def _nsa_prep_kernel(pos_ref, inv_ref, q_ref, kv0_ref, kv1_ref, kv2_ref, qo_ref, kvo_ref, *, scale):
    ang = pos_ref[...] * inv_ref[...]
    lane = lax.broadcasted_iota(jnp.int32, ang.shape, 1)
    cos2 = jnp.cos(ang)
    sin = jnp.sin(ang)
    sin2 = jnp.where(lane < HALF, -sin, sin)

    def rot(x):
        return x * cos2 + pltpu.roll(x, HALF, 1) * sin2

    for h in range(NSA_HEADS):
        sl = slice(h * NSA_HD, (h + 1) * NSA_HD)
        qo_ref[:, sl] = (rot(q_ref[:, sl]) * scale).astype(qo_ref.dtype)
    for p in range(N_KV_PLANES):
        src = (kv0_ref, kv1_ref, kv2_ref)[p // 4]
        x = src[:, (p % 4) * NSA_HD:(p % 4 + 1) * NSA_HD]
        if (p // NSA_KV_GROUPS) % 2 == 0:
            x = rot(x)
        kvo_ref[p] = x.astype(kvo_ref.dtype)


def nsa_prep(proj, pos_f, inv2, q_col, kv_col, *, tm=512):
    T = proj.shape[0]
    assert T % tm == 0 and q_col % NSA_WIDTH == 0 and kv_col % (4 * NSA_HD) == 0
    qb = q_col // NSA_WIDTH
    kb = kv_col // (4 * NSA_HD)
    return pl.pallas_call(
        functools.partial(_nsa_prep_kernel, scale=NSA_HD ** -0.5),
        grid=(T // tm,),
        in_specs=[
            pl.BlockSpec((tm, 1), lambda i: (i, 0)),
            pl.BlockSpec((1, NSA_HD), lambda i: (0, 0)),
            pl.BlockSpec((tm, NSA_WIDTH), lambda i: (i, qb)),
            pl.BlockSpec((tm, 4 * NSA_HD), lambda i: (i, kb)),
            pl.BlockSpec((tm, 4 * NSA_HD), lambda i: (i, kb + 1)),
            pl.BlockSpec((tm, 4 * NSA_HD), lambda i: (i, kb + 2)),
        ],
        out_specs=[
            pl.BlockSpec((tm, NSA_WIDTH), lambda i: (i, 0)),
            pl.BlockSpec((N_KV_PLANES, tm, NSA_HD), lambda i: (0, i, 0)),
        ],
        out_shape=[
            jax.ShapeDtypeStruct((T, NSA_WIDTH), jnp.bfloat16),
            jax.ShapeDtypeStruct((N_KV_PLANES, T, NSA_HD), jnp.bfloat16),
        ],
        compiler_params=pltpu.CompilerParams(dimension_semantics=("arbitrary",), vmem_limit_bytes=VMEM_LIMIT),
        name="nsa_prep",
    )(pos_f, inv2, proj, proj, proj, proj)


def _nsa_cmp_kernel(t2_ref, pe_ref, w1_ref, w2_ref, o_ref):
    t2 = t2_ref[0, 0]
    half = t2.shape[1]
    n = t2.shape[0]
    a = jnp.dot(t2, w1_ref[0, :half, :], preferred_element_type=jnp.float32)
    b = jnp.dot(t2, w1_ref[0, half:, :], preferred_element_type=jnp.float32)
    pe8 = jnp.broadcast_to(pe_ref[0], (8, 2 * half)).astype(jnp.bfloat16)
    bias = jnp.dot(pe8, w1_ref[0], preferred_element_type=jnp.float32)[0:1]
    pre = a + pltpu.roll(b, n - 1, 0) + bias
    hid = jax.nn.gelu(pre)
    o_ref[0, 0, 0] = jnp.dot(hid.astype(jnp.bfloat16), w2_ref[0], preferred_element_type=jnp.float32)


def nsa_compress(kv_planes, pe, w1, w2, B, S):
    ncp = S // CMP_STRIDE
    t2 = kv_planes.reshape(N_KV_PLANES, B, ncp, CMP_STRIDE * NSA_HD)
    kdim = CMP_BLOCK * NSA_HD
    G = NSA_KV_GROUPS
    return pl.pallas_call(
        _nsa_cmp_kernel,
        grid=(2, B, G),
        in_specs=[
            pl.BlockSpec((1, 1, ncp, CMP_STRIDE * NSA_HD), lambda c, b, g: (c * G + g, b, 0, 0)),
            pl.BlockSpec((1, 1, kdim), lambda c, b, g: (c, 0, 0)),
            pl.BlockSpec((1, kdim, CMP_HIDDEN), lambda c, b, g: (c, 0, 0)),
            pl.BlockSpec((1, CMP_HIDDEN, NSA_HD), lambda c, b, g: (c, 0, 0)),
        ],
        out_specs=pl.BlockSpec((1, 1, 1, ncp, NSA_HD), lambda c, b, g: (c, b, g, 0, 0)),
        out_shape=jax.ShapeDtypeStruct((2, B, G, ncp, NSA_HD), jnp.float32),
        compiler_params=pltpu.CompilerParams(
            dimension_semantics=("arbitrary", "arbitrary", "arbitrary"), vmem_limit_bytes=VMEM_LIMIT),
        name="nsa_compress",
    )(t2, pe.reshape(2, 1, kdim), w1.astype(jnp.bfloat16), w2.astype(jnp.bfloat16))


KT = 128


def _flash_step(q4, k, v, ok, carry, reps):
    m_i, l_i, acc = carry
    s = lax.dot_general(q4, k, NT_DIMS, preferred_element_type=jnp.float32)
    ok4 = jnp.concatenate([ok] * reps, axis=0)
    sm = jnp.where(ok4, s, NEG_INF)
    m_new = jnp.maximum(m_i, jnp.max(sm, axis=-1, keepdims=True))
    alpha = jnp.exp(m_i - m_new)
    p = jnp.where(ok4, jnp.exp(sm - m_new), 0.0)
    l_new = alpha * l_i + jnp.sum(p, axis=-1, keepdims=True)
    acc = alpha * acc + jnp.dot(p.astype(jnp.bfloat16), v, preferred_element_type=jnp.float32)
    return m_new, l_new, acc


def _nsa_attn_kernel(q_ref, kc_ref, vc_ref, ks_ref, vs_ref, kw_ref, vw_ref, gate_ref, ov_ref, o_ref, *, tq):
    f32, bf16 = jnp.float32, jnp.bfloat16
    R = NSA_HEADS // NSA_KV_GROUPS
    i = pl.program_id(2)
    q = q_ref[...]
    q4 = jnp.concatenate([q[:, r * NSA_HD:(r + 1) * NSA_HD] for r in range(R)], axis=0)
    t = i * tq + lax.broadcasted_iota(jnp.int32, (tq, 1), 0)

    kc = kc_ref[0, 0, 0].astype(bf16)
    vc = vc_ref[0, 0, 0].astype(bf16)
    ncp = kc.shape[0]
    s = lax.dot_general(q4, kc, NT_DIMS, preferred_element_type=f32)
    j = lax.broadcasted_iota(jnp.int32, (1, ncp), 1)
    last = jnp.where(j < ncp - 1, j * CMP_STRIDE + (CMP_BLOCK - 1), jnp.int32(2 ** 30))
    valid = jnp.concatenate([last <= t] * R, axis=0)
    sm = jnp.where(valid, s, NEG_INF)
    e = jnp.where(valid, jnp.exp(sm - jnp.max(sm, axis=-1, keepdims=True)), 0.0)
    den = jnp.sum(e, axis=-1, keepdims=True)
    p = e * jnp.where(den > 0.0, 1.0 / den, 0.0)
    o_cmp = jnp.dot(p.astype(bf16), vc, preferred_element_type=f32)
    psum = p[0:tq]
    for r in range(1, R):
        psum = psum + p[r * tq:(r + 1) * tq]
    p_hi = psum.astype(bf16)
    p_lo = (psum - p_hi.astype(f32)).astype(bf16)
    imp = (jnp.dot(p_hi, ov_ref[...], preferred_element_type=f32)
           + jnp.dot(p_lo, ov_ref[...], preferred_element_type=f32))

    n_slc = imp.shape[1]
    n = lax.broadcasted_iota(jnp.int32, (1, n_slc), 1)
    cur = t >> 6
    forced = jnp.where(n == 0, 1.0, 0.0) + jnp.where(n == cur, 1.0, 0.0) + jnp.where(n == cur - 1, 1.0, 0.0)
    sc = jnp.where(n <= cur, jnp.where(forced > 0.0, FORCED_SCORE, imp), -1.0)
    rank = jnp.zeros(sc.shape, f32)
    for m in range(n_slc):
        col = sc[:, m:m + 1]
        rank = rank + jnp.where(n > m, jnp.where(col >= sc, 1.0, 0.0), jnp.where(col > sc, 1.0, 0.0))
    sel = jnp.where(rank < float(SEL_TOPK), jnp.where(sc >= 0.0, 1.0, 0.0), 0.0).astype(bf16)

    lane = lax.broadcasted_iota(jnp.int32, (1, KT), 1)
    e_row = lax.broadcasted_iota(jnp.int32, (n_slc, KT), 0)
    e_lane = lax.broadcasted_iota(jnp.int32, (n_slc, KT), 1)
    init = (jnp.full((R * tq, 1), NEG_INF, f32), jnp.zeros((R * tq, 1), f32), jnp.zeros((R * tq, NSA_HD), f32))

    def slc_body(kb, carry):
        k0 = pl.multiple_of(kb * KT, KT)
        expand = jnp.where(e_row == kb * (KT // SEL_BLOCK) + e_lane // SEL_BLOCK, 1.0, 0.0).astype(bf16)
        sel_k = jnp.dot(sel, expand, preferred_element_type=f32)
        ok = jnp.where(k0 + lane <= t, sel_k, 0.0) > 0.5
        return _flash_step(q4, ks_ref[0, pl.ds(k0, KT), :], vs_ref[0, pl.ds(k0, KT), :], ok, carry, R)

    n_kt = (i * tq + tq) // KT
    _, l_s, acc_s = lax.fori_loop(0, n_kt, slc_body, init)
    o_slc = acc_s / l_s

    def win_body(kb, carry):
        k0 = pl.multiple_of(kb * KT, KT)
        dist = t - (k0 + lane)
        ok = dist.astype(jnp.uint32) < jnp.uint32(WINDOW)
        return _flash_step(q4, kw_ref[0, pl.ds(k0, KT), :], vw_ref[0, pl.ds(k0, KT), :], ok, carry, R)

    first_kt = jnp.maximum(i * tq - WINDOW, 0) // KT
    _, l_w, acc_w = lax.fori_loop(first_kt, n_kt, win_body, init)
    o_win = acc_w / l_w

    gt = jax.nn.sigmoid(gate_ref[...])
    for r in range(R):
        rows = slice(r * tq, (r + 1) * tq)
        o = (gt[:, 3 * r:3 * r + 1] * o_cmp[rows] + gt[:, 3 * r + 1:3 * r + 2] * o_slc[rows]
             + gt[:, 3 * r + 2:3 * r + 3] * o_win[rows])
        o_ref[:, r * NSA_HD:(r + 1) * NSA_HD] = o.astype(o_ref.dtype)


def nsa_attention(q_r, kv_planes, cmp_kv, proj, gate_col, B, S, *, tq=128):
    T = B * S
    G = NSA_KV_GROUPS
    R = NSA_HEADS // G
    ncp = S // CMP_STRIDE
    n_slc = S // SEL_BLOCK
    assert S % tq == 0 and tq % KT == 0 and gate_col % LANES == 0
    nqt = S // tq
    gb = gate_col // LANES
    cstart = np.arange(ncp)[:, None] * CMP_STRIDE
    sstart = np.arange(n_slc)[None, :] * SEL_BLOCK
    overlap = jnp.asarray((cstart < sstart + SEL_BLOCK) & (cstart + CMP_BLOCK > sstart), jnp.bfloat16)

    def plane(c):
        return pl.BlockSpec((1, S, NSA_HD), lambda b, g, i: (c * G + g, b, 0))

    def cmp_spec(c):
        return pl.BlockSpec((1, 1, 1, ncp, NSA_HD), lambda b, g, i: (c, b, g, 0, 0))

    return pl.pallas_call(
        functools.partial(_nsa_attn_kernel, tq=tq),
        grid=(B, G, nqt),
        in_specs=[
            pl.BlockSpec((tq, R * NSA_HD), lambda b, g, i: (b * nqt + i, g)),
            cmp_spec(0), cmp_spec(1),
            plane(2), plane(3), plane(4), plane(5),
            pl.BlockSpec((tq, LANES), lambda b, g, i: (b * nqt + i, gb + g)),
            pl.BlockSpec((ncp, n_slc), lambda b, g, i: (0, 0)),
        ],
        out_specs=pl.BlockSpec((tq, R * NSA_HD), lambda b, g, i: (b * nqt + i, g)),
        out_shape=jax.ShapeDtypeStruct((T, NSA_WIDTH), jnp.bfloat16),
        compiler_params=pltpu.CompilerParams(
            dimension_semantics=("arbitrary", "arbitrary", "arbitrary"), vmem_limit_bytes=VMEM_LIMIT),
        name="nsa_attention",
    )(q_r, cmp_kv, cmp_kv, kv_planes, kv_planes, kv_planes, kv_planes, proj, overlap)


def nsa_from_proj(proj, positions, cmp_pe, cmp_w1, cmp_w2, B, S, q_col, kv_col, gate_col):
    T = B * S
    pos_f = positions.reshape(T, 1).astype(jnp.float32)
    inv = ROPE_THETA ** (-jnp.arange(0, NSA_HD, 2, dtype=jnp.float32) / NSA_HD)
    inv2 = jnp.concatenate([inv, inv]).reshape(1, NSA_HD)
    q_r, kv_planes = nsa_prep(proj, pos_f, inv2, q_col, kv_col)
    cmp_kv = nsa_compress(kv_planes, cmp_pe, cmp_w1, cmp_w2, B, S)
    return nsa_attention(q_r, kv_planes, cmp_kv, proj, gate_col, B, S)


N_TOK_EXPERTS = PEER_HEADS * PEER_TOPK
SUBKEY_DIM = PEER_DKEY // 2
N_HALVES = 2 * PEER_HEADS
CAND_SMALL = 8


def _extract_top(vals, tags, big, count):
    out_v, out_t = [], []
    for _ in range(count):
        mx = jnp.max(vals, axis=0, keepdims=True)
        tag = jnp.min(jnp.where(vals == mx, tags, big), axis=0, keepdims=True)
        out_v.append(mx)
        out_t.append(tag)
        vals = jnp.where(tags == tag, -jnp.inf, vals)
    return out_v, out_t


def _peer_topk_kernel(h_ref, wq_ref, sk_ref, spread_ref, ids_ref, g_ref, q_scr, hv_scr, hi_scr, e_scr, s_scr):
    f32, bf16, i32 = jnp.float32, jnp.bfloat16, jnp.int32
    tm = h_ref.shape[0]
    q = jnp.dot(h_ref[...].astype(bf16), wq_ref[...], preferred_element_type=f32)
    for hc in range(N_HALVES):
        q_scr[hc] = q[:, hc * SUBKEY_DIM:(hc + 1) * SUBKEY_DIM].astype(bf16)
    key_id = lax.broadcasted_iota(i32, (PEER_NKEYS, tm), 0)

    def stage1(hc, carry):
        scores = lax.dot_general(sk_ref[hc], q_scr[hc], NT_DIMS, preferred_element_type=f32)
        vs, ks = _extract_top(scores, key_id, PEER_NKEYS, PEER_TOPK)
        hv_scr[hc] = jnp.concatenate(vs, axis=0)
        hi_scr[hc] = jnp.concatenate(ks, axis=0)
        return carry

    lax.fori_loop(0, N_HALVES, stage1, 0)

    b_all = lax.broadcasted_iota(i32, (PEER_TOPK, tm), 0)
    b_small = lax.broadcasted_iota(i32, (CAND_SMALL, tm), 0)

    def stage2(h, carry):
        v0, v1 = hv_scr[2 * h], hv_scr[2 * h + 1]
        i0, i1 = hi_scr[2 * h], hi_scr[2 * h + 1]
        cs = [v0[0:1] + v1]
        ce = [i0[0:1] * PEER_NKEYS + i1]
        cf = [b_all]
        for a in range(1, CAND_SMALL):
            cs.append(v0[a:a + 1] + v1[0:CAND_SMALL])
            ce.append(i0[a:a + 1] * PEER_NKEYS + i1[0:CAND_SMALL])
            cf.append(a * PEER_TOPK + b_small)
        cs.append(v0[CAND_SMALL:] + v1[0:1])
        ce.append(i0[CAND_SMALL:] * PEER_NKEYS + i1[0:1])
        cf.append((CAND_SMALL + b_small) * PEER_TOPK)
        cand = jnp.concatenate(cs, axis=0)
        eid = jnp.concatenate(ce, axis=0)
        flat = jnp.concatenate(cf, axis=0)
        ts, te = [], []
        for _ in range(PEER_TOPK):
            mx = jnp.max(cand, axis=0, keepdims=True)
            first = jnp.min(jnp.where(cand == mx, flat, PEER_TOPK * PEER_TOPK), axis=0, keepdims=True)
            hit = flat == first
            te.append(jnp.max(jnp.where(hit, eid, -1), axis=0, keepdims=True))
            ts.append(mx)
            cand = jnp.where(hit, -jnp.inf, cand)
        top_s = jnp.concatenate(ts, axis=0)
        ex = jnp.exp(top_s - top_s[0:1])
        rows = pl.ds(pl.multiple_of(h * PEER_TOPK, PEER_TOPK), PEER_TOPK)
        s_scr[rows, :] = ex / jnp.sum(ex, axis=0, keepdims=True)
        e_scr[rows, :] = jnp.concatenate(te, axis=0)
        return carry

    lax.fori_loop(0, PEER_HEADS, stage2, 0)

    ids_ref[...] = e_scr[...].T
    gt = s_scr[...].T
    g_hi = gt.astype(bf16)
    g_lo = (gt - g_hi.astype(f32)).astype(bf16)
    g_ref[...] = (jnp.dot(g_hi, spread_ref[...], preferred_element_type=f32)
                  + jnp.dot(g_lo, spread_ref[...], preferred_element_type=f32))


def peer_topk(h, wq, subkeys, *, tm=256):
    T, D = h.shape
    assert T % tm == 0
    nq = wq.shape[1]
    spread = jnp.asarray(np.arange(2 * N_TOK_EXPERTS)[None, :] == 2 * np.arange(N_TOK_EXPERTS)[:, None], jnp.bfloat16)
    sk = subkeys.reshape(N_HALVES, PEER_NKEYS, SUBKEY_DIM).astype(jnp.bfloat16)
    return pl.pallas_call(
        _peer_topk_kernel,
        grid=(T // tm,),
        in_specs=[
            pl.BlockSpec((tm, D), lambda i: (i, 0)),
            pl.BlockSpec((D, nq), lambda i: (0, 0)),
            pl.BlockSpec((N_HALVES, PEER_NKEYS, SUBKEY_DIM), lambda i: (0, 0, 0)),
            pl.BlockSpec((N_TOK_EXPERTS, 2 * N_TOK_EXPERTS), lambda i: (0, 0)),
        ],
        out_specs=[
            pl.BlockSpec((tm, N_TOK_EXPERTS), lambda i: (i, 0)),
            pl.BlockSpec((tm, 2 * N_TOK_EXPERTS), lambda i: (i, 0)),
        ],
        out_shape=[
            jax.ShapeDtypeStruct((T, N_TOK_EXPERTS), jnp.int32),
            jax.ShapeDtypeStruct((T, 2 * N_TOK_EXPERTS), jnp.float32),
        ],
        scratch_shapes=[
            pltpu.VMEM((N_HALVES, tm, SUBKEY_DIM), jnp.bfloat16),
            pltpu.VMEM((N_HALVES, PEER_TOPK, tm), jnp.float32),
            pltpu.VMEM((N_HALVES, PEER_TOPK, tm), jnp.int32),
            pltpu.VMEM((N_TOK_EXPERTS, tm), jnp.int32),
            pltpu.VMEM((N_TOK_EXPERTS, tm), jnp.float32),
        ],
        compiler_params=pltpu.CompilerParams(dimension_semantics=("arbitrary",), vmem_limit_bytes=VMEM_LIMIT),
        name="peer_topk",
    )(h, wq, sk, spread)


N_SLOTS = 16
PHASE = 4
CHUNK = 128


def _peer_expert_kernel(ids_ref, x_ref, g_ref, lg_ref, lb_ref, tab_ref, o_ref, *scratch):
    bufs, ffn_scr, sems = scratch[:N_SLOTS], scratch[N_SLOTS], scratch[N_SLOTS + 1]
    f32, bf16 = jnp.float32, jnp.bfloat16
    tb, D = x_ref.shape
    half = D // 2
    n_rows = 2 * N_TOK_EXPERTS

    def issue(t, slot, k_lo=0, k_hi=N_TOK_EXPERTS):
        for k in range(k_lo, k_hi):
            pltpu.make_async_copy(tab_ref.at[pl.ds(ids_ref[t, k], 1), :], bufs[slot].at[pl.ds(k, 1), :],
                                  sems.at[slot]).start(priority=k % 2)

    def wait(slot):
        pltpu.make_async_copy(bufs[slot], bufs[slot], sems.at[slot]).wait()

    n_phases = N_SLOTS // PHASE
    for j in range(N_SLOTS - PHASE):
        issue(j, j)

    pad = jnp.zeros((14, CHUNK), f32)
    padw = jnp.zeros((14, n_rows), f32)
    n_chunks = half // CHUNK
    per_chunk = N_TOK_EXPERTS // n_chunks

    def group(it, last):
        t0 = pl.multiple_of(it * N_SLOTS, N_SLOTS)
        for p in range(n_phases):
            slots = list(range(p * PHASE, (p + 1) * PHASE))
            refill = list(range((p - 1) % n_phases * PHASE, ((p - 1) % n_phases + 1) * PHASE))
            prefetch = not (last and p > 0)
            for j in slots:
                wait(j)
            rows = [pl.ds(t0 + j, 1) for j in slots]
            pres = []
            for n, j in enumerate(slots):
                ahead = t0 + refill[n] if p == 0 else t0 + N_SLOTS + refill[n]
                x = x_ref[rows[n], :]
                d = jnp.zeros((16, n_rows), f32)
                for c in range(n_chunks):
                    lo = slice(c * CHUNK, (c + 1) * CHUNK)
                    hi = slice(half + c * CHUNK, half + (c + 1) * CHUNK)
                    x2 = jnp.concatenate([x[:, lo], x[:, hi], pad], axis=0).astype(bf16)
                    wc = pltpu.bitcast(bufs[j][:, lo], bf16)
                    d = d + lax.dot_general(x2, wc, NT_DIMS, preferred_element_type=f32)
                    if prefetch:
                        issue(ahead, refill[n], c * per_chunk, (c + 1) * per_chunk)
                pres.append(d)
            for n, j in enumerate(slots):
                d = pres[n]
                pre = d[0:1] + pltpu.roll(d[1:2], n_rows - 1, 1)
                act = 0.5 * pre * (1.0 + lax.erf(pre * (2.0 ** -0.5)))
                wa = act * g_ref[rows[n], :]
                w2 = jnp.concatenate([wa, pltpu.roll(wa, 1, 1), padw], axis=0).astype(bf16)
                wv = pltpu.bitcast(bufs[j][:, half:], bf16)
                o = jnp.dot(w2, wv, preferred_element_type=f32)
                ffn_scr[rows[n], :half] = o[0:1]
                ffn_scr[rows[n], half:] = o[1:2]

    def body(it, carry):
        group(it, False)
        return carry

    n_groups = tb // N_SLOTS
    lax.fori_loop(0, n_groups - 1, body, 0)
    group(n_groups - 1, True)
    o_ref[...] = _layer_norm(DEEPNORM_ALPHA * x_ref[...] + ffn_scr[...], lg_ref[...], lb_ref[...])


def peer_pack_table(u, v):
    def pack(w):
        bits = lax.bitcast_convert_type(w.astype(jnp.bfloat16), jnp.uint16).astype(jnp.uint32)
        hw = w.shape[1] // 2
        return bits[:, :hw] | (bits[:, hw:] << 16)
    return jnp.concatenate([pack(u), pack(v)], axis=1)


def peer_experts_ln(h, ids, gates, table, ln_g, ln_b, *, tb=256):
    T, D = h.shape
    assert T % tb == 0 and tb % N_SLOTS == 0 and tb >= 2 * N_SLOTS
    return pl.pallas_call(
        _peer_expert_kernel,
        grid=(T // tb,),
        in_specs=[
            pl.BlockSpec((tb, N_TOK_EXPERTS), lambda i: (i, 0), memory_space=pltpu.SMEM),
            pl.BlockSpec((tb, D), lambda i: (i, 0)),
            pl.BlockSpec((tb, 2 * N_TOK_EXPERTS), lambda i: (i, 0)),
            pl.BlockSpec((1, D), lambda i: (0, 0)),
            pl.BlockSpec((1, D), lambda i: (0, 0)),
            pl.BlockSpec(memory_space=pl.ANY),
        ],
        out_specs=pl.BlockSpec((tb, D), lambda i: (i, 0)),
        out_shape=jax.ShapeDtypeStruct((T, D), jnp.float32),
        scratch_shapes=[pltpu.VMEM((N_TOK_EXPERTS, D), jnp.uint32) for _ in range(N_SLOTS)] + [
            pltpu.VMEM((tb, D), jnp.float32),
            pltpu.SemaphoreType.DMA((N_SLOTS,)),
        ],
        compiler_params=pltpu.CompilerParams(dimension_semantics=("arbitrary",), vmem_limit_bytes=VMEM_LIMIT),
        name="peer_experts",
    )(ids, h, gates, ln_g.reshape(1, D), ln_b.reshape(1, D), table)


def _hgrn_kernel(lbl_ref, ng_ref, q_ref, f_ref, i_ref, g_ref, o_ref, state_ref, *, layer):
    f32, bf16 = jnp.float32, jnp.bfloat16
    ts = q_ref.shape[0]
    C = HG_CHUNK
    n_chunks = ts // C

    @pl.when(pl.program_id(2) == 0)
    def _():
        state_ref[...] = jnp.zeros_like(state_ref)

    logits = lbl_ref[...]
    ex = jnp.exp(logits - jnp.max(logits, axis=0, keepdims=True))
    sm = ex / jnp.sum(ex, axis=0, keepdims=True)
    lb = jnp.sum(sm[0:layer + 1], axis=0, keepdims=True)

    fz = f_ref[...]
    logf = jnp.log(lb + (1.0 - lb) * jax.nn.sigmoid(fz))
    k = (1.0 - lb) * jax.nn.sigmoid(-fz)
    qf = jax.nn.silu(q_ref[...])
    v = i_ref[...]

    pos = lax.broadcasted_iota(jnp.int32, (ts, 1), 0) % C
    b = logf
    shift = 1
    while shift < C:
        b = b + jnp.where(pos >= shift, pltpu.roll(b, shift, 0), 0.0)
        shift *= 2
    b3 = b.reshape(n_chunks, C, HG_DK)
    b_last = b3[:, C - 1:C, :]
    q_dec = (qf * jnp.exp(b)).astype(bf16)
    k_dec = (k * jnp.exp(-b)).astype(bf16)
    k_st = (k.reshape(n_chunks, C, HG_DK) * jnp.exp(b_last - b3)).astype(bf16)
    e_last = jnp.exp(b_last)
    vb = v.astype(bf16)
    causal = (lax.broadcasted_iota(jnp.int32, (C, C), 0) >= lax.broadcasted_iota(jnp.int32, (C, C), 1))

    state_t = state_ref[...]
    outs = []
    for c in range(n_chunks):
        rows = slice(c * C, (c + 1) * C)
        qc, kc, vc = q_dec[rows], k_dec[rows], vb[rows]
        a = lax.dot_general(qc, kc, NT_DIMS, preferred_element_type=f32)
        a = jnp.where(causal, a, 0.0).astype(bf16)
        o = (jnp.dot(a, vc, preferred_element_type=f32)
             + lax.dot_general(qc, state_t.astype(bf16), NT_DIMS, preferred_element_type=f32))
        outs.append(o)
        upd = jnp.dot(vc.T, k_st[c], preferred_element_type=f32)
        state_t = e_last[c] * state_t + upd
    state_ref[...] = state_t
    o = jnp.concatenate(outs, axis=0)
    o = o * lax.rsqrt(jnp.mean(o * o, axis=-1, keepdims=True) + 1e-6) * ng_ref[...]
    o_ref[...] = (o * jax.nn.silu(g_ref[...])).astype(o_ref.dtype)


def hgrn2_from_proj(proj, lb_logits, norm_g, layer, B, S, *, ts=512):
    T = B * S
    assert S % ts == 0 and ts % HG_CHUNK == 0 and HG_DK == LANES and HG_DV == LANES
    ns = S // ts
    H = HG_HEADS
    nl = lb_logits.shape[0]

    def col(part):
        return pl.BlockSpec((ts, LANES), lambda b, h, s: (b * ns + s, part * H + h))

    return pl.pallas_call(
        functools.partial(_hgrn_kernel, layer=layer),
        grid=(B, H, ns),
        in_specs=[
            pl.BlockSpec((nl, LANES), lambda b, h, s: (0, h)),
            pl.BlockSpec((1, LANES), lambda b, h, s: (0, h)),
            col(0), col(1), col(2), col(3),
        ],
        out_specs=pl.BlockSpec((ts, LANES), lambda b, h, s: (b * ns + s, h)),
        out_shape=jax.ShapeDtypeStruct((T, HG_VWIDTH), jnp.bfloat16),
        scratch_shapes=[pltpu.VMEM((HG_DV, HG_DK), jnp.float32)],
        compiler_params=pltpu.CompilerParams(
            dimension_semantics=("arbitrary", "arbitrary", "arbitrary"), vmem_limit_bytes=VMEM_LIMIT),
        name="hgrn2",
    )(lb_logits, norm_g.reshape(1, HG_VWIDTH), proj, proj, proj, proj)


def _merge_kernel(oa_ref, ob_ref, ga_ref, gb_ref, wa_ref, wb_ref, o_ref):
    up_a = jnp.dot(oa_ref[...], wa_ref[...], preferred_element_type=jnp.float32)
    up_b = jnp.dot(ob_ref[...], wb_ref[...], preferred_element_type=jnp.float32)
    o_ref[...] = (jax.nn.sigmoid(ga_ref[...]) * up_a + jax.nn.sigmoid(gb_ref[...]) * up_b).astype(o_ref.dtype)


def gated_merge(o_a, o_b, proj, ga_col, gb_col, w_up_a, w_up_b, *, tm=512, tn=512):
    T, Ka = o_a.shape
    Kb = o_b.shape[1]
    N = w_up_a.shape[1]
    assert T % tm == 0 and N % tn == 0 and ga_col % tn == 0 and gb_col % tn == 0
    return pl.pallas_call(
        _merge_kernel,
        grid=(T // tm, N // tn),
        in_specs=[
            pl.BlockSpec((tm, Ka), lambda i, j: (i, 0)),
            pl.BlockSpec((tm, Kb), lambda i, j: (i, 0)),
            pl.BlockSpec((tm, tn), lambda i, j: (i, ga_col // tn + j)),
            pl.BlockSpec((tm, tn), lambda i, j: (i, gb_col // tn + j)),
            pl.BlockSpec((Ka, tn), lambda i, j: (0, j)),
            pl.BlockSpec((Kb, tn), lambda i, j: (0, j)),
        ],
        out_specs=pl.BlockSpec((tm, tn), lambda i, j: (i, j)),
        out_shape=jax.ShapeDtypeStruct((T, N), jnp.bfloat16),
        compiler_params=pltpu.CompilerParams(
            dimension_semantics=("arbitrary", "arbitrary"), vmem_limit_bytes=VMEM_LIMIT),
        name="gated_merge",
    )(o_a, o_b, proj, proj, w_up_a, w_up_b)


def _out_ln_kernel(m_ref, x_ref, w_ref, g0_ref, b0_ref, g1_ref, b1_ref, o_ref):
    mix = jnp.dot(m_ref[...], w_ref[...], preferred_element_type=jnp.float32)
    h = _layer_norm(x_ref[...], g0_ref[...], b0_ref[...])
    o_ref[...] = _layer_norm(DEEPNORM_ALPHA * h + mix, g1_ref[...], b1_ref[...])


def out_proj_ln(merged, x, w_out, g0, b0, g1, b1, *, tm=256):
    T, D = x.shape
    assert T % tm == 0
    row = pl.BlockSpec((1, D), lambda i: (0, 0))
    return pl.pallas_call(
        _out_ln_kernel,
        grid=(T // tm,),
        in_specs=[
            pl.BlockSpec((tm, D), lambda i: (i, 0)),
            pl.BlockSpec((tm, D), lambda i: (i, 0)),
            pl.BlockSpec((D, D), lambda i: (0, 0)),
            row, row, row, row,
        ],
        out_specs=pl.BlockSpec((tm, D), lambda i: (i, 0)),
        out_shape=jax.ShapeDtypeStruct((T, D), jnp.float32),
        compiler_params=pltpu.CompilerParams(dimension_semantics=("arbitrary",), vmem_limit_bytes=VMEM_LIMIT),
        name="out_proj_ln",
    )(merged, x, w_out, g0.reshape(1, D), b0.reshape(1, D), g1.reshape(1, D), b1.reshape(1, D))


def kernel(x, positions, ln_in_g, ln_in_b, w_in, hgrn_lb_logits, hgrn_norm_g, cmp_pe, cmp_w1, cmp_w2,
           w_up_hgrn, w_up_nsa, w_out, ln1_g, ln1_b, peer_wq, peer_subkeys, peer_u, peer_v, ln2_g, ln2_b):
    B, S, D = x.shape
    T = B * S
    bf16 = jnp.bfloat16
    xt = x.reshape(T, D)
    layer = 0

    o = np.cumsum((0,) + IN_SPLITS)
    w = w_in[layer]
    n_gate_g = IN_SPLITS[6] // NSA_KV_GROUPS
    gate_tiles = []
    for g in range(NSA_KV_GROUPS):
        gate_tiles += [w[:, o[6] + g * n_gate_g:o[6] + (g + 1) * n_gate_g], jnp.zeros((D, LANES - n_gate_g), w.dtype)]
    n_used = o[6] + 2 * D + NSA_KV_GROUPS * LANES
    n_pad = -n_used % IN_TN
    w_r = jnp.concatenate([w[:, o[0]:o[6]], w[:, o[7]:o[9]]] + gate_tiles + [jnp.zeros((D, n_pad), w.dtype)],
                          axis=1).astype(bf16)
    proj = ln_matmul(xt, ln_in_g, ln_in_b, w_r, tn=IN_TN)
    c = np.cumsum((0, HG_WIDTH, HG_WIDTH, HG_VWIDTH, HG_VWIDTH, NSA_WIDTH, 6 * NSA_KV_WIDTH, D, D))

    o_a = hgrn2_from_proj(proj, hgrn_lb_logits, hgrn_norm_g[layer], layer, B, S)
    o_b = nsa_from_proj(proj, positions, cmp_pe[layer], cmp_w1[layer], cmp_w2[layer], B, S,
                        q_col=int(c[4]), kv_col=int(c[5]), gate_col=int(c[8]))
    merged = gated_merge(o_a, o_b, proj, int(c[6]), int(c[7]),
                         w_up_hgrn[layer].astype(bf16), w_up_nsa[layer].astype(bf16))
    h = out_proj_ln(merged, xt, w_out[layer].astype(bf16), ln_in_g, ln_in_b, ln1_g[layer], ln1_b[layer])

    ids, gates = peer_topk(h, peer_wq[layer].astype(bf16), peer_subkeys[layer])
    table = peer_pack_table(peer_u[layer], peer_v[layer])
    h = peer_experts_ln(h, ids, gates, table, ln2_g[layer], ln2_b[layer])
    return h.reshape(B, S, D)
```
